```python
import jax, jax.numpy as jnp
from jax import lax
import numpy as np

D_MODEL = 1024
BATCH = 8
SEQ = 4096
DEPTH = 2

N_EVEN = (DEPTH + 1) // 2
N_ODD = DEPTH // 2

RET_HEADS = 4
RET_DK = D_MODEL // (2 * RET_HEADS)
RET_DV = D_MODEL // (2 * RET_HEADS)
RET_CHUNK = 128
ROPE_THETA = 10000.0

LRU_WIDTH = D_MODEL // 2
LRU_BLOCKS = 8
LRU_BLOCK = LRU_WIDTH // LRU_BLOCKS
LRU_C = 8.0
CONV_K = 4

GDN_HEADS = 8
GDN_DK = D_MODEL // GDN_HEADS
GDN_DV = D_MODEL // GDN_HEADS
GDN_CHUNK = 64

D_FF = 4 * D_MODEL
EPS = 1e-6

RET_QK = RET_HEADS * RET_DK
RET_V = RET_HEADS * RET_DV
EVEN_SPLITS = (RET_QK, RET_QK, RET_V, RET_V, LRU_WIDTH, LRU_WIDTH)
EVEN_IN = sum(EVEN_SPLITS)
EVEN_MIX = RET_V + LRU_WIDTH
GDN_K = GDN_HEADS * GDN_DK
GDN_V = GDN_HEADS * GDN_DV
GDN_CONV_DIM = 2 * GDN_K + GDN_V
ODD_SPLITS = (GDN_K, GDN_K, GDN_V, GDN_V, GDN_HEADS, GDN_HEADS)
ODD_IN = sum(ODD_SPLITS)

kernel_name = "hybrid_retention_rglru_gdn_trunk"


def _split(p, sizes):
    offs = np.cumsum(sizes)[:-1].tolist()
    return jnp.split(p, offs, axis=-1)


def rms_norm(x, w):
    xf = x.astype(jnp.float32)
    return xf * lax.rsqrt(jnp.mean(xf * xf, axis=-1, keepdims=True) + EPS) * w.astype(jnp.float32)


def head_rms(x):
    return x * lax.rsqrt(jnp.mean(x * x, axis=-1, keepdims=True) + EPS)


def causal_depthwise_conv(x, w, b=None):
    K, C = w.shape
    y = lax.conv_general_dilated(x, w[:, None, :].astype(x.dtype), window_strides=(1,),
                                 padding=[(K - 1, 0)], dimension_numbers=('NWC', 'WIO', 'NWC'),
                                 feature_group_count=C)
    if b is not None:
        y = y + b.astype(x.dtype)
    return y


def rope(x, pos):
    half = x.shape[-1] // 2
    inv = ROPE_THETA ** (-jnp.arange(half, dtype=jnp.float32) / half)
    ang = pos.astype(jnp.float32)[:, None] * inv[None, :]
    cos = jnp.cos(ang)[None, :, None, :]
    sin = jnp.sin(ang)[None, :, None, :]
    x1, x2 = x[..., :half], x[..., half:]
    return jnp.concatenate([x1 * cos - x2 * sin, x1 * sin + x2 * cos], axis=-1)


def retention(q, k, v):
    B, T, H, dk = q.shape
    dv = v.shape[-1]
    C = RET_CHUNK
    N = T // C
    log_gamma = jnp.log1p(-jnp.exp2(-5.0 - jnp.arange(H, dtype=jnp.float32)))
    k = k * (dk ** -0.5)
    q = q.reshape(B, N, C, H, dk)
    k = k.reshape(B, N, C, H, dk)
    v = v.reshape(B, N, C, H, dv)
    idx = jnp.arange(C, dtype=jnp.float32)
    diff = idx[:, None] - idx[None, :]
    causal = diff >= 0
    decay = jnp.where(causal, jnp.exp(log_gamma[:, None, None] * jnp.where(causal, diff, 0.0)), 0.0)
    scores = jnp.einsum('bnihd,bnjhd->bnhij', q, k) * decay
    o_intra = jnp.einsum('bnhij,bnjhe->bnihe', scores, v)
    q_decay = jnp.exp(log_gamma[None, :] * (idx[:, None] + 1.0))
    k_decay = jnp.exp(log_gamma[None, :] * (C - 1.0 - idx[:, None]))
    chunk_gamma = jnp.exp(log_gamma * C)
    kv = jnp.einsum('bnjhd,bnjhe->bnhde', k * k_decay[:, :, None], v)

    def step(S, kv_n):
        return chunk_gamma[None, :, None, None] * S + kv_n, S

    S0 = jnp.zeros((B, H, dk, dv), jnp.float32)
    _, S_prev = lax.scan(step, S0, jnp.moveaxis(kv, 1, 0))
    S_prev = jnp.moveaxis(S_prev, 0, 1)
    o_inter = jnp.einsum('bnihd,bnhde->bnihe', q * q_decay[:, :, None], S_prev)
    return (o_intra + o_inter).reshape(B, T, H, dv)


def rg_lru(x, w_r, b_r, w_i, b_i, lam):
    B, T, W = x.shape
    xb = x.reshape(B, T, LRU_BLOCKS, LRU_BLOCK)
    r = jax.nn.sigmoid(jnp.einsum('btnd,nde->btne', xb, w_r.astype(jnp.float32)).reshape(B, T, W)
                       + b_r.astype(jnp.float32))
    i = jax.nn.sigmoid(jnp.einsum('btnd,nde->btne', xb, w_i.astype(jnp.float32)).reshape(B, T, W)
                       + b_i.astype(jnp.float32))
    log_a = -LRU_C * r * jax.nn.softplus(-lam.astype(jnp.float32))
    a = jnp.exp(log_a)
    mult = jnp.sqrt(-jnp.expm1(2.0 * log_a))
    mult = jnp.where(jnp.arange(T)[None, :, None] == 0, 1.0, mult)
    u = x * i * mult

    def combine(left, right):
        a1, b1 = left
        a2, b2 = right
        return a1 * a2, a2 * b1 + b2

    _, h = lax.associative_scan(combine, (a, u), axis=1)
    return h


def gated_delta_rule(q, k, v, g, beta):
    B, T, H, dk = q.shape
    dv = v.shape[-1]
    C = GDN_CHUNK
    N = T // C
    q = q * (dk ** -0.5)
    to_chunks = lambda t: jnp.moveaxis(t.reshape((B, N, C, H) + t.shape[3:]), 3, 1)
    q, k, v, g, beta = map(to_chunks, (q, k, v, g, beta))
    G = jnp.cumsum(g, axis=-1)
    tri_incl = jnp.tril(jnp.ones((C, C), bool))
    tri_strict = jnp.tril(jnp.ones((C, C), bool), k=-1)
    gdiff = G[..., :, None] - G[..., None, :]
    decay_incl = jnp.where(tri_incl, jnp.exp(jnp.where(tri_incl, gdiff, 0.0)), 0.0)
    decay_strict = jnp.where(tri_strict, decay_incl, 0.0)
    k_beta = k * beta[..., None]
    v_beta = v * beta[..., None]
    L = jnp.einsum('bhnid,bhnjd->bhnij', k_beta, k) * decay_strict
    A = jnp.eye(C, dtype=jnp.float32) + L
    u = lax.linalg.triangular_solve(A, v_beta, left_side=True, lower=True)
    w = lax.linalg.triangular_solve(A, k_beta * jnp.exp(G)[..., None], left_side=True, lower=True)
    qk = jnp.einsum('bhnid,bhnjd->bhnij', q, k) * decay_incl
    q_g = q * jnp.exp(G)[..., None]
    k_g = k * jnp.exp(G[..., -1:] - G)[..., None]
    chunk_decay = jnp.exp(G[..., -1])

    def step(S, xs):
        u_n, w_n, qk_n, qg_n, kg_n, cd_n = xs
        v_new = u_n - jnp.einsum('bhcd,bhde->bhce', w_n, S)
        o = jnp.einsum('bhcd,bhde->bhce', qg_n, S) + jnp.einsum('bhij,bhje->bhie', qk_n, v_new)
        S = S * cd_n[..., None, None] + jnp.einsum('bhcd,bhce->bhde', kg_n, v_new)
        return S, o

    xs = tuple(jnp.moveaxis(t, 2, 0) for t in (u, w, qk, q_g, k_g, chunk_decay))
    S0 = jnp.zeros((B, H, dk, dv), jnp.float32)
    _, o = lax.scan(step, S0, xs)
    return jnp.transpose(o, (1, 0, 3, 2, 4)).reshape(B, T, H, dv)


def retention_rglru_mixer(h, pos, w_in, lru_conv_w, lru_conv_b, lru_w_r, lru_b_r,
                          lru_w_i, lru_b_i, lru_lambda, w_out):
    B, T, _ = h.shape
    p = h @ w_in.astype(jnp.float32)
    q, k, v, g_ret, x_lru, y_lru = _split(p, EVEN_SPLITS)
    q = rope(q.reshape(B, T, RET_HEADS, RET_DK), pos)
    k = rope(k.reshape(B, T, RET_HEADS, RET_DK), pos)
    v = v.reshape(B, T, RET_HEADS, RET_DV)
    o_ret = head_rms(retention(q, k, v)).reshape(B, T, RET_V) * jax.nn.silu(g_ret)
    x_lru = causal_depthwise_conv(x_lru, lru_conv_w, lru_conv_b)
    o_lru = rg_lru(x_lru, lru_w_r, lru_b_r, lru_w_i, lru_b_i, lru_lambda) * jax.nn.gelu(y_lru)
    return jnp.concatenate([o_ret, o_lru], axis=-1) @ w_out.astype(jnp.float32)


def gated_deltanet_mixer(h, w_in, conv_w, a_log, dt_bias, norm_w, w_out):
    B, T, _ = h.shape
    p = h @ w_in.astype(jnp.float32)
    qkv, z, b, a = _split(p, (GDN_CONV_DIM, GDN_V, GDN_HEADS, GDN_HEADS))
    qkv = jax.nn.silu(causal_depthwise_conv(qkv, conv_w))
    q, k, v = _split(qkv, (GDN_K, GDN_K, GDN_V))
    q = q.reshape(B, T, GDN_HEADS, GDN_DK)
    k = k.reshape(B, T, GDN_HEADS, GDN_DK)
    v = v.reshape(B, T, GDN_HEADS, GDN_DV)
    q = q * lax.rsqrt(jnp.sum(q * q, axis=-1, keepdims=True) + EPS)
    k = k * lax.rsqrt(jnp.sum(k * k, axis=-1, keepdims=True) + EPS)
    beta = jax.nn.sigmoid(b)
    g = -jnp.exp(a_log.astype(jnp.float32)) * jax.nn.softplus(a + dt_bias.astype(jnp.float32))
    o = gated_delta_rule(q, k, v, g, beta)
    o = head_rms(o) * norm_w.astype(jnp.float32) * jax.nn.silu(z.reshape(B, T, GDN_HEADS, GDN_DV))
    return o.reshape(B, T, GDN_V) @ w_out.astype(jnp.float32)


def squared_relu_mlp(h, w_up, w_down):
    return jnp.square(jax.nn.relu(h @ w_up.astype(jnp.float32))) @ w_down.astype(jnp.float32)


def setup_inputs(seed: int = 0) -> dict:
    key = jax.random.key(seed)
    ks = jax.random.split(key, 24)
    f32 = jnp.float32
    nrm = lambda k, shape, scale: jax.random.normal(k, shape, f32) * scale
    x = nrm(ks[0], (BATCH, SEQ, D_MODEL), 1.0)
    mixer_norm_w = 1.0 + nrm(ks[1], (DEPTH, D_MODEL), 0.02)
    mlp_norm_w = 1.0 + nrm(ks[2], (DEPTH, D_MODEL), 0.02)
    final_norm_w = 1.0 + nrm(ks[3], (D_MODEL,), 0.02)
    w_in_even = nrm(ks[4], (N_EVEN, D_MODEL, EVEN_IN), D_MODEL ** -0.5)
    lru_conv_w = nrm(ks[5], (N_EVEN, CONV_K, LRU_WIDTH), CONV_K ** -0.5)
    lru_conv_b = nrm(ks[6], (N_EVEN, LRU_WIDTH), 0.01)
    lru_w_r = nrm(ks[7], (N_EVEN, LRU_BLOCKS, LRU_BLOCK, LRU_BLOCK), LRU_BLOCK ** -0.5)
    lru_b_r = nrm(ks[8], (N_EVEN, LRU_WIDTH), 0.01)
    lru_w_i = nrm(ks[9], (N_EVEN, LRU_BLOCKS, LRU_BLOCK, LRU_BLOCK), LRU_BLOCK ** -0.5)
    lru_b_i = nrm(ks[10], (N_EVEN, LRU_WIDTH), 0.01)
    a_c = jax.random.uniform(ks[11], (N_EVEN, LRU_WIDTH), f32, 0.9, 0.999)
    s = a_c ** (1.0 / LRU_C)
    lru_lambda = jnp.log(s) - jnp.log1p(-s)
    w_out_even = nrm(ks[12], (N_EVEN, EVEN_MIX, D_MODEL), EVEN_MIX ** -0.5)
    w_in_odd = nrm(ks[13], (N_ODD, D_MODEL, ODD_IN), D_MODEL ** -0.5)
    gdn_conv_w = nrm(ks[14], (N_ODD, CONV_K, GDN_CONV_DIM), CONV_K ** -0.5)
    gdn_a_log = jnp.log(jax.random.uniform(ks[15], (N_ODD, GDN_HEADS), f32, 1.0, 16.0))
    dt = jnp.exp(jax.random.uniform(ks[16], (N_ODD, GDN_HEADS), f32,
                                    float(np.log(1e-3)), float(np.log(1e-1))))
    gdn_dt_bias = dt + jnp.log(-jnp.expm1(-dt))
    gdn_norm_w = 1.0 + nrm(ks[17], (N_ODD, GDN_DV), 0.02)
    w_out_odd = nrm(ks[18], (N_ODD, GDN_V, D_MODEL), GDN_V ** -0.5)
    w_up = nrm(ks[19], (DEPTH, D_MODEL, D_FF), D_MODEL ** -0.5)
    w_down = nrm(ks[20], (DEPTH, D_FF, D_MODEL), D_FF ** -0.5)
    return {"x": x, "mixer_norm_w": mixer_norm_w, "mlp_norm_w": mlp_norm_w,
            "final_norm_w": final_norm_w, "w_in_even": w_in_even, "lru_conv_w": lru_conv_w,
            "lru_conv_b": lru_conv_b, "lru_w_r": lru_w_r, "lru_b_r": lru_b_r,
            "lru_w_i": lru_w_i, "lru_b_i": lru_b_i, "lru_lambda": lru_lambda,
            "w_out_even": w_out_even, "w_in_odd": w_in_odd, "gdn_conv_w": gdn_conv_w,
            "gdn_a_log": gdn_a_log, "gdn_dt_bias": gdn_dt_bias, "gdn_norm_w": gdn_norm_w,
            "w_out_odd": w_out_odd, "w_up": w_up, "w_down": w_down}


def reference(x, mixer_norm_w, mlp_norm_w, final_norm_w, w_in_even, lru_conv_w, lru_conv_b,
              lru_w_r, lru_b_r, lru_w_i, lru_b_i, lru_lambda, w_out_even, w_in_odd,
              gdn_conv_w, gdn_a_log, gdn_dt_bias, gdn_norm_w, w_out_odd, w_up, w_down):
    pos = jnp.arange(x.shape[1], dtype=jnp.int32)
    for layer in range(DEPTH):
        j = layer // 2
        h = rms_norm(x, mixer_norm_w[layer])
        if layer % 2 == 0:
            mix = retention_rglru_mixer(h, pos, w_in_even[j], lru_conv_w[j], lru_conv_b[j],
                                        lru_w_r[j], lru_b_r[j], lru_w_i[j], lru_b_i[j],
                                        lru_lambda[j], w_out_even[j])
        else:
            mix = gated_deltanet_mixer(h, w_in_odd[j], gdn_conv_w[j], gdn_a_log[j],
                                       gdn_dt_bias[j], gdn_norm_w[j], w_out_odd[j])
        x = x + mix.astype(x.dtype)
        x = x + squared_relu_mlp(rms_norm(x, mlp_norm_w[layer]), w_up[layer], w_down[layer]).astype(x.dtype)
    return rms_norm(x, final_norm_w).astype(x.dtype)
```

```python
import functools

import numpy as np
import jax
import jax.numpy as jnp
from jax import lax
from jax.experimental import pallas as pl
from jax.experimental.pallas import tpu as pltpu

EPS = 1e-6
ROPE_THETA = 10000.0
LRU_C = 8.0
RET_HEADS = 4
GDN_HEADS = 8
LRU_BLOCKS = 8
CONV_K = 4
HEAD_DIM = 128
RET_CHUNK = 128
GDN_CHUNK = 64
GDN_SUB = 16
LANES = 128
SUBLANES = 8
VMEM_LIMIT = 56 * 1024 * 1024

BF = jnp.bfloat16
F32 = jnp.float32


def _mm(a, b):
    return jnp.dot(a.astype(BF), b.astype(BF), preferred_element_type=F32)


def _mm_nt(a, b):
    return lax.dot_general(a.astype(BF), b.astype(BF), (((1,), (1,)), ((), ())),
                           preferred_element_type=F32)


def _mm_tn(a, b):
    return lax.dot_general(a.astype(BF), b.astype(BF), (((0,), (0,)), ((), ())),
                           preferred_element_type=F32)


def _mm_f32(a, b):
    return jnp.dot(a, b, preferred_element_type=F32, precision=lax.Precision.HIGHEST)


def _rms(x, w):
    return x * lax.rsqrt(jnp.mean(x * x, axis=-1, keepdims=True) + EPS) * w


def _silu(x):
    return x * jax.nn.sigmoid(x)


def _softplus(x):
    return jnp.maximum(x, 0.0) + jnp.log(1.0 + jnp.exp(-jnp.abs(x)))


def _params(*sem):
    return pltpu.CompilerParams(dimension_semantics=sem, vmem_limit_bytes=VMEM_LIMIT)


def _norm_proj_kernel(x_ref, nw_ref, w_ref, *out_refs, col_chunk):
    hb = _rms(x_ref[...], nw_ref[...]).astype(BF)
    off = 0
    for o_ref in out_refs:
        n = o_ref.shape[-1]
        for j in range(0, n, col_chunk):
            cw = min(col_chunk, n - j)
            o_ref[:, j:j + cw] = jnp.dot(hb, w_ref[:, off + j:off + j + cw],
                                         preferred_element_type=F32).astype(o_ref.dtype)
        off += n


def _norm_proj(x2, nw, w, splits, tm=512):
    M, D = x2.shape
    Nw = w.shape[1]
    assert sum(splits) == Nw and M % tm == 0
    out_shape = [jax.ShapeDtypeStruct((M, n), F32) for n in splits]
    out_specs = [pl.BlockSpec((tm, n), lambda i: (i, 0)) for n in splits]
    return pl.pallas_call(
        functools.partial(_norm_proj_kernel, col_chunk=512),
        grid=(M // tm,),
        in_specs=[pl.BlockSpec((tm, D), lambda i: (i, 0)),
                  pl.BlockSpec((1, D), lambda i: (0, 0)),
                  pl.BlockSpec((D, Nw), lambda i: (0, 0))],
        out_specs=out_specs,
        out_shape=out_shape,
        compiler_params=_params("parallel"),
        name="norm_proj",
    )(x2, nw.reshape(1, D), w)


def _outproj_mlp_kernel(*refs, n_parts, final, ff_chunk):
    parts = refs[:n_parts]
    x_ref, wout_ref, nw_ref, wup_ref, wdown_ref = refs[n_parts:n_parts + 5]
    fnw_ref = refs[n_parts + 5] if final else None
    o_ref = refs[-1]
    x1 = x_ref[...]
    off = 0
    for p in parts:
        kd = p.shape[-1]
        x1 = x1 + jnp.dot(p[...].astype(BF), wout_ref[off:off + kd, :],
                          preferred_element_type=F32)
        off += kd
    hb = _rms(x1, nw_ref[...]).astype(BF)
    o_ref[...] = x1
    dff = wup_ref.shape[1]
    for j in range(0, dff, ff_chunk):
        u = jnp.dot(hb, wup_ref[:, j:j + ff_chunk], preferred_element_type=F32)
        a = jnp.square(jnp.maximum(u, 0.0)).astype(BF)
        o_ref[...] += jnp.dot(a, wdown_ref[j:j + ff_chunk, :], preferred_element_type=F32)
    if final:
        o_ref[...] = _rms(o_ref[...], fnw_ref[...])


def _outproj_mlp(parts, x2, wout, nw, wup, wdown, fnw=None, tm=512):
    M, D = x2.shape
    dff = wup.shape[1]
    final = fnw is not None
    in_specs = [pl.BlockSpec((tm, p.shape[1]), lambda i: (i, 0)) for p in parts]
    in_specs += [pl.BlockSpec((tm, D), lambda i: (i, 0)),
                 pl.BlockSpec(wout.shape, lambda i: (0, 0)),
                 pl.BlockSpec((1, D), lambda i: (0, 0)),
                 pl.BlockSpec((D, dff), lambda i: (0, 0)),
                 pl.BlockSpec((dff, D), lambda i: (0, 0))]
    args = list(parts) + [x2, wout, nw.reshape(1, D), wup, wdown]
    if final:
        in_specs.append(pl.BlockSpec((1, D), lambda i: (0, 0)))
        args.append(fnw.reshape(1, D))
    return pl.pallas_call(
        functools.partial(_outproj_mlp_kernel, n_parts=len(parts), final=final, ff_chunk=1024),
        grid=(M // tm,),
        in_specs=in_specs,
        out_specs=pl.BlockSpec((tm, D), lambda i: (i, 0)),
        out_shape=jax.ShapeDtypeStruct((M, D), F32),
        compiler_params=_params("parallel"),
        name="outproj_mlp",
    )(*args)


def _retention_kernel(q_ref, k_ref, v_ref, g_ref, cos_ref, sin_ref, dec_ref, qd_ref, kd_ref,
                      cg_ref, o_ref, *, chunk):
    T, dk = q_ref.shape
    dec = dec_ref[...]
    qd = qd_ref[...]
    kd = kd_ref[...]
    cg = cg_ref[...]
    scale = dk ** -0.5
    half = dk // 2

    def rope(x, cos, sin):
        return x * cos + pltpu.roll(x, half, 1) * sin

    def body(n, S):
        r = pl.ds(pl.multiple_of(n * chunk, chunk), chunk)
        cos = cos_ref[r, :]
        sin = sin_ref[r, :]
        q = rope(q_ref[r, :], cos, sin)
        k = rope(k_ref[r, :], cos, sin) * scale
        v = v_ref[r, :]
        s = _mm_nt(q, k) * dec
        o = _mm(s, v) + _mm(q * qd, S)
        S = cg * S + _mm_tn(k * kd, v)
        o = o * lax.rsqrt(jnp.mean(o * o, axis=-1, keepdims=True) + EPS)
        o_ref[r, :] = o * _silu(g_ref[r, :])
        return S

    lax.fori_loop(0, T // chunk, body, jnp.zeros((dk, dk), F32))


def _retention(p, B, T, chunk=RET_CHUNK):
    H, dk, C = RET_HEADS, HEAD_DIM, chunk
    half = dk // 2
    inv = ROPE_THETA ** (-jnp.arange(half, dtype=F32) / half)
    ang = jnp.arange(T, dtype=F32)[:, None] * inv[None, :]
    cos = jnp.concatenate([jnp.cos(ang), jnp.cos(ang)], axis=-1)
    sin = jnp.concatenate([-jnp.sin(ang), jnp.sin(ang)], axis=-1)
    log_gamma = jnp.log1p(-jnp.exp2(-5.0 - jnp.arange(H, dtype=F32)))
    idx = jnp.arange(C, dtype=F32)
    diff = idx[:, None] - idx[None, :]
    causal = diff >= 0
    dec = jnp.where(causal, jnp.exp(log_gamma[:, None, None] * jnp.where(causal, diff, 0.0)), 0.0)
    qd = jnp.broadcast_to(jnp.exp(log_gamma[:, None] * (idx[None, :] + 1.0))[:, :, None], (H, C, dk))
    kd = jnp.broadcast_to(jnp.exp(log_gamma[:, None] * (C - 1.0 - idx[None, :]))[:, :, None], (H, C, dk))
    cg = jnp.broadcast_to(jnp.exp(log_gamma * C)[:, None, None], (H, 1, dk))

    def col(g):
        return pl.BlockSpec((None, T, dk), lambda b, h: (b, 0, g * H + h))

    tab = pl.BlockSpec((T, dk), lambda b, h: (0, 0))
    return pl.pallas_call(
        functools.partial(_retention_kernel, chunk=C),
        grid=(B, H),
        in_specs=[col(0), col(1), col(2), col(3), tab, tab,
                  pl.BlockSpec((None, C, C), lambda b, h: (h, 0, 0)),
                  pl.BlockSpec((None, C, dk), lambda b, h: (h, 0, 0)),
                  pl.BlockSpec((None, C, dk), lambda b, h: (h, 0, 0)),
                  pl.BlockSpec((None, 1, dk), lambda b, h: (h, 0, 0))],
        out_specs=pl.BlockSpec((None, T, dk), lambda b, h: (b, 0, h)),
        out_shape=jax.ShapeDtypeStruct((B, T, H * dk), F32),
        compiler_params=_params("parallel", "parallel"),
        name="retention",
    )(p, p, p, p, cos, sin, dec, qd, kd, cg)


def _scan_rows(a, b):
    R = a.shape[0]
    row = lax.broadcasted_iota(jnp.int32, a.shape, 0)
    d = 1
    while d < R:
        keep = row >= d
        a_s = jnp.where(keep, pltpu.roll(a, d, 0), 1.0)
        b_s = jnp.where(keep, pltpu.roll(b, d, 0), 0.0)
        b = a * b_s + b
        a = a * a_s
        d *= 2
    return a, b


def _rglru_kernel(x_ref, y_ref, cw_ref, cb_ref, wg_ref, bg_ref, lam_ref, o_ref,
                  xpad_ref, h_ref, *, sub):
    tt, W = x_ref.shape
    t = pl.program_id(1)

    @pl.when(t == 0)
    def _():
        xpad_ref[0:SUBLANES, :] = jnp.zeros((SUBLANES, W), F32)
        h_ref[...] = jnp.zeros(h_ref.shape, F32)

    xpad_ref[SUBLANES:SUBLANES + tt, :] = x_ref[...]
    neg_c_sp = -LRU_C * _softplus(-lam_ref[...])
    cw = cw_ref[...]
    cb = cb_ref[...]
    bg = bg_ref[...]
    first = t == 0
    for r0 in range(0, tt, sub):
        xc = cb + cw[0:1, :] * xpad_ref[r0 + 5:r0 + 5 + sub, :]
        for kk in range(1, CONV_K):
            xc = xc + cw[kk:kk + 1, :] * xpad_ref[r0 + 5 + kk:r0 + 5 + kk + sub, :]
        gates = _mm(xc, wg_ref[...]) + bg
        r = jax.nn.sigmoid(gates[:, :W])
        i = jax.nn.sigmoid(gates[:, W:])
        log_a = neg_c_sp * r
        a = jnp.exp(log_a)
        th = jnp.tanh(log_a)
        mult = jnp.sqrt(-2.0 * th / (1.0 - th))
        if r0 == 0:
            row = lax.broadcasted_iota(jnp.int32, mult.shape, 0)
            mult = jnp.where(jnp.logical_and(first, row == 0), 1.0, mult)
        u = xc * i * mult
        acum, hloc = _scan_rows(a, u)
        h = hloc + acum * h_ref[...]
        h_ref[...] = h[sub - 1:sub, :]
        o_ref[r0:r0 + sub, :] = h * jax.nn.gelu(y_ref[r0:r0 + sub, :])
    xpad_ref[0:SUBLANES, :] = xpad_ref[tt:tt + SUBLANES, :]


def _rglru(p, B, T, cw, cb, wg, bg, lam, tt=1024, sub=256):
    W = cw.shape[1]
    tt = min(tt, T)
    return pl.pallas_call(
        functools.partial(_rglru_kernel, sub=sub),
        grid=(B, T // tt),
        in_specs=[pl.BlockSpec((None, tt, W), lambda b, t: (b, t, 4)),
                  pl.BlockSpec((None, tt, W), lambda b, t: (b, t, 5)),
                  pl.BlockSpec((CONV_K, W), lambda b, t: (0, 0)),
                  pl.BlockSpec((1, W), lambda b, t: (0, 0)),
                  pl.BlockSpec((W, 2 * W), lambda b, t: (0, 0)),
                  pl.BlockSpec((1, 2 * W), lambda b, t: (0, 0)),
                  pl.BlockSpec((1, W), lambda b, t: (0, 0))],
        out_specs=pl.BlockSpec((None, tt, W), lambda b, t: (b, t, 0)),
        out_shape=jax.ShapeDtypeStruct((B, T, W), F32),
        scratch_shapes=[pltpu.VMEM((tt + SUBLANES, W), F32), pltpu.VMEM((1, W), F32)],
        compiler_params=_params("parallel", "arbitrary"),
        name="rglru",
    )(p, p, cw, cb.reshape(1, W), wg, bg.reshape(1, 2 * W), lam.reshape(1, W))


def _unit_lower_inverse(L, ii, jj):
    C = L.shape[0]
    eye = (ii == jj).astype(F32)
    same = (ii // GDN_SUB) == (jj // GDN_SUB)
    Ld = jnp.where(same, L, 0.0)
    Lo = L - Ld
    P = eye - Ld
    M = Ld
    d = 2
    while d < GDN_SUB:
        M = _mm_f32(M, M)
        P = _mm_f32(P, eye + M)
        d *= 2
    Nm = _mm_f32(P, Lo)
    R = eye - Nm
    M = Nm
    d = 2
    while d < C // GDN_SUB:
        M = _mm_f32(M, M)
        R = _mm_f32(R, eye + M)
        d *= 2
    return _mm_f32(R, P)


def _gdn_kernel(q_ref, k_ref, v_ref, z_ref, cwq_ref, cwk_ref, cwv_ref,
                a_nc_ref, a_cn_ref, b_cn_ref, alog_ref, dtb_ref, nw_ref, o_ref,
                pad_ref, qn_ref, kn_ref, vn_ref, u_ref, w_ref, qk_ref, qg_ref, kg_ref,
                cd_ref, gnc_ref, *, chunk, tile):
    T, dk = q_ref.shape
    C = chunk
    N = T // C

    pad_ref[0:SUBLANES, :] = jnp.zeros((SUBLANES, dk), F32)
    for src, cw_ref, dst, l2, scale in ((q_ref, cwq_ref, qn_ref, True, dk ** -0.5),
                                        (k_ref, cwk_ref, kn_ref, True, 1.0),
                                        (v_ref, cwv_ref, vn_ref, False, 1.0)):
        pad_ref[SUBLANES:SUBLANES + T, :] = src[...]
        cw = cw_ref[...]
        for r0 in range(0, T, tile):
            y = cw[0:1, :] * pad_ref[r0 + 5:r0 + 5 + tile, :]
            for kk in range(1, CONV_K):
                y = y + cw[kk:kk + 1, :] * pad_ref[r0 + 5 + kk:r0 + 5 + kk + tile, :]
            y = _silu(y)
            if l2:
                y = y * (lax.rsqrt(jnp.sum(y * y, axis=-1, keepdims=True) + EPS) * scale)
            dst[r0:r0 + tile, :] = y

    neg_a = -jnp.exp(alog_ref[...])
    dtb = dtb_ref[...]
    g_nc = neg_a * _softplus(a_nc_ref[...] + dtb)
    g_cn = neg_a * _softplus(a_cn_ref[...] + dtb)
    beta_cn = jax.nn.sigmoid(b_cn_ref[...])
    ii = lax.broadcasted_iota(jnp.int32, (C, C), 0)
    jj = lax.broadcasted_iota(jnp.int32, (C, C), 1)
    incl = ii >= jj
    strict = ii > jj
    tril = incl.astype(F32)
    triu = (ii <= jj).astype(F32)
    gnc_ref[...] = _mm_f32(g_nc, triu)
    G_cn = _mm_f32(tril, g_cn)
    lane_n = lax.broadcasted_iota(jnp.int32, (C, N), 1)
    lane_c = lax.broadcasted_iota(jnp.int32, (1, C), 1)

    def prep(n, carry):
        r = pl.ds(pl.multiple_of(n * C, C), C)
        q = qn_ref[r, :]
        k = kn_ref[r, :]
        v = vn_ref[r, :]
        pick = lane_n == n
        gcol = jnp.sum(jnp.where(pick, G_cn, 0.0), axis=1, keepdims=True)
        bcol = jnp.sum(jnp.where(pick, beta_cn, 0.0), axis=1, keepdims=True)
        grow = gnc_ref[pl.ds(n, 1), :]
        glast = jnp.sum(jnp.where(lane_c == C - 1, grow, 0.0), axis=1, keepdims=True)
        dm = jnp.where(incl, jnp.exp(jnp.where(incl, gcol - grow, 0.0)), 0.0)
        kb = k * bcol
        L = jnp.where(strict, _mm_nt(kb, k) * dm, 0.0)
        tinv = _unit_lower_inverse(L, ii, jj)
        eg = jnp.exp(gcol)
        u_ref[r, :] = _mm_f32(tinv, v * bcol)
        w_ref[r, :] = _mm_f32(tinv, kb * eg)
        qk_ref[r, :] = _mm_nt(q, k) * dm
        qg_ref[r, :] = q * eg
        kg_ref[r, :] = k * jnp.exp(glast - gcol)
        cd_ref[pl.ds(n, 1), :] = jnp.broadcast_to(jnp.exp(glast), (1, dk))
        return carry

    lax.fori_loop(0, N, prep, 0)

    nw = nw_ref[...]

    def step(n, S):
        r = pl.ds(pl.multiple_of(n * C, C), C)
        v_new = u_ref[r, :] - _mm(w_ref[r, :], S)
        o = _mm(qg_ref[r, :], S) + _mm(qk_ref[r, :], v_new)
        S = S * cd_ref[pl.ds(n, 1), :] + _mm_tn(kg_ref[r, :], v_new)
        o = o * lax.rsqrt(jnp.mean(o * o, axis=-1, keepdims=True) + EPS)
        o_ref[r, :] = o * nw * _silu(z_ref[r, :])
        return S

    lax.fori_loop(0, N, step, jnp.zeros((dk, dk), F32))


def _gdn(p, ab, conv_w, a_log, dt_bias, norm_w, B, T, chunk=GDN_CHUNK):
    H, dk, C = GDN_HEADS, HEAD_DIM, chunk
    N = T // C
    bl = jnp.swapaxes(ab[..., :H], 1, 2).reshape(B, H, N, C)
    al = jnp.swapaxes(ab[..., H:2 * H], 1, 2).reshape(B, H, N, C)
    a_nc = al
    a_cn = jnp.swapaxes(al, 2, 3)
    b_cn = jnp.swapaxes(bl, 2, 3)

    def col(g):
        return pl.BlockSpec((None, T, dk), lambda b, h: (b, 0, g * H + h))

    def cwcol(g):
        return pl.BlockSpec((CONV_K, dk), lambda b, h: (0, g * H + h))

    nc = pl.BlockSpec((None, None, N, C), lambda b, h: (b, h, 0, 0))
    cn = pl.BlockSpec((None, None, C, N), lambda b, h: (b, h, 0, 0))
    sc = pl.BlockSpec((None, 1, 1), lambda b, h: (h, 0, 0))
    tbuf = pltpu.VMEM((T, dk), F32)
    return pl.pallas_call(
        functools.partial(_gdn_kernel, chunk=C, tile=512),
        grid=(B, H),
        in_specs=[col(0), col(1), col(2), col(3), cwcol(0), cwcol(1), cwcol(2),
                  nc, cn, cn, sc, sc, pl.BlockSpec((1, dk), lambda b, h: (0, 0))],
        out_specs=pl.BlockSpec((None, T, dk), lambda b, h: (b, 0, h)),
        out_shape=jax.ShapeDtypeStruct((B, T, H * dk), F32),
        scratch_shapes=[pltpu.VMEM((T + SUBLANES, dk), F32),
                        tbuf, tbuf, tbuf,
                        tbuf, tbuf,
                        pltpu.VMEM((T, C), F32),
                        tbuf, tbuf,
                        pltpu.VMEM((N, dk), F32),
                        pltpu.VMEM((N, C), F32)],
        compiler_params=_params("parallel", "parallel"),
        name="gated_delta",
    )(p, p, p, p, conv_w, conv_w, conv_w, a_nc, a_cn, b_cn,
      a_log.reshape(H, 1, 1), dt_bias.reshape(H, 1, 1), norm_w.reshape(1, dk))


def _block_diag(w):
    n, d, _ = w.shape
    eye = jnp.eye(n, dtype=w.dtype)
    return (eye[:, None, :, None] * w[:, :, None, :]).reshape(n * d, n * d)


def kernel(x, mixer_norm_w, mlp_norm_w, final_norm_w, w_in_even, lru_conv_w, lru_conv_b,
           lru_w_r, lru_b_r, lru_w_i, lru_b_i, lru_lambda, w_out_even, w_in_odd, gdn_conv_w,
           gdn_a_log, gdn_dt_bias, gdn_norm_w, w_out_odd, w_up, w_down):
    B, T, D = x.shape
    M = B * T
    x2 = x.reshape(M, D)

    n_even = w_in_even.shape[2]
    (p0,) = _norm_proj(x2, mixer_norm_w[0], w_in_even[0].astype(BF), (n_even,))
    p0 = p0.reshape(B, T, n_even)
    o_ret = _retention(p0, B, T)
    wg = jnp.concatenate([_block_diag(lru_w_r[0]), _block_diag(lru_w_i[0])], axis=1).astype(BF)
    bg = jnp.concatenate([lru_b_r[0], lru_b_i[0]])
    o_lru = _rglru(p0, B, T, lru_conv_w[0], lru_conv_b[0], wg, bg, lru_lambda[0])
    x2 = _outproj_mlp([o_ret.reshape(M, -1), o_lru.reshape(M, -1)], x2,
                      w_out_even[0].astype(BF), mlp_norm_w[0],
                      w_up[0].astype(BF), w_down[0].astype(BF))

    n_main = 4 * GDN_HEADS * HEAD_DIM
    w1 = w_in_odd[0]
    w1 = jnp.pad(w1, ((0, 0), (0, n_main + LANES - w1.shape[1]))).astype(BF)
    p1, ab = _norm_proj(x2, mixer_norm_w[1], w1, (n_main, LANES))
    o_gdn = _gdn(p1.reshape(B, T, n_main), ab.reshape(B, T, LANES), gdn_conv_w[0],
                 gdn_a_log[0], gdn_dt_bias[0], gdn_norm_w[0], B, T)
    x2 = _outproj_mlp([o_gdn.reshape(M, -1)], x2, w_out_odd[0].astype(BF), mlp_norm_w[1],
                      w_up[1].astype(BF), w_down[1].astype(BF), fnw=final_norm_w)
    return x2.reshape(B, T, D)
```

```python
import functools

import numpy as np
import jax
import jax.numpy as jnp
from jax import lax
from jax.experimental import pallas as pl
from jax.experimental.pallas import tpu as pltpu

EPS = 1e-6
ROPE_THETA = 10000.0
LRU_C = 8.0
RET_HEADS = 4
GDN_HEADS = 8
LRU_BLOCKS = 8
CONV_K = 4
HEAD_DIM = 128
RET_CHUNK = 128
GDN_CHUNK = 64
GDN_SUB = 16
GDN_GROUP = 256
GDN_TILE = 512
GDN_HEADS_PER_ITER = 2
LANES = 128
SUBLANES = 8
VMEM_LIMIT = 56 * 1024 * 1024

BF = jnp.bfloat16
F32 = jnp.float32


def _mm(a, b):
    return jnp.dot(a.astype(BF), b.astype(BF), preferred_element_type=F32)


def _mm_nt(a, b):
    return lax.dot_general(a.astype(BF), b.astype(BF), (((1,), (1,)), ((), ())),
                           preferred_element_type=F32)


def _mm_tn(a, b):
    return lax.dot_general(a.astype(BF), b.astype(BF), (((0,), (0,)), ((), ())),
                           preferred_element_type=F32)


def _mm_f32(a, b):
    return jnp.dot(a, b, preferred_element_type=F32, precision=lax.Precision.HIGHEST)


def _mm_nt_f32(a, b):
    return lax.dot_general(a, b, (((1,), (1,)), ((), ())), preferred_element_type=F32,
                           precision=lax.Precision.HIGHEST)


def _rms(x, w):
    return x * lax.rsqrt(jnp.mean(x * x, axis=-1, keepdims=True) + EPS) * w


def _silu(x):
    hx = 0.5 * x
    return hx + hx * jnp.tanh(hx)


def _softplus(x):
    return jnp.maximum(x, 0.0) + jnp.log(1.0 + jnp.exp(-jnp.abs(x)))


def _params(*sem):
    return pltpu.CompilerParams(dimension_semantics=sem, vmem_limit_bytes=VMEM_LIMIT)


def _norm_proj_kernel(x_ref, nw_ref, w_ref, *out_refs, col_chunk):
    hb = _rms(x_ref[...], nw_ref[...]).astype(BF)
    off = 0
    for o_ref in out_refs:
        n = o_ref.shape[-1]
        for j in range(0, n, col_chunk):
            cw = min(col_chunk, n - j)
            o_ref[:, j:j + cw] = jnp.dot(hb, w_ref[:, off + j:off + j + cw],
                                         preferred_element_type=F32).astype(o_ref.dtype)
        off += n


def _norm_proj(x2, nw, w, splits, tm=512):
    M, D = x2.shape
    Nw = w.shape[1]
    assert sum(splits) == Nw and M % tm == 0
    out_shape = [jax.ShapeDtypeStruct((M, n), F32) for n in splits]
    out_specs = [pl.BlockSpec((tm, n), lambda i: (i, 0)) for n in splits]
    return pl.pallas_call(
        functools.partial(_norm_proj_kernel, col_chunk=512),
        grid=(M // tm,),
        in_specs=[pl.BlockSpec((tm, D), lambda i: (i, 0)),
                  pl.BlockSpec((1, D), lambda i: (0, 0)),
                  pl.BlockSpec((D, Nw), lambda i: (0, 0))],
        out_specs=out_specs,
        out_shape=out_shape,
        compiler_params=_params("parallel"),
        name="norm_proj",
    )(x2, nw.reshape(1, D), w)


def _outproj_mlp_kernel(*refs, n_parts, final, ff_chunk):
    parts = refs[:n_parts]
    x_ref, wout_ref, nw_ref, wup_ref, wdown_ref = refs[n_parts:n_parts + 5]
    fnw_ref = refs[n_parts + 5] if final else None
    o_ref = refs[-1]
    x1 = x_ref[...]
    off = 0
    for p in parts:
        kd = p.shape[-1]
        x1 = x1 + jnp.dot(p[...].astype(BF), wout_ref[off:off + kd, :],
                          preferred_element_type=F32)
        off += kd
    hb = _rms(x1, nw_ref[...]).astype(BF)
    o_ref[...] = x1
    dff = wup_ref.shape[1]
    for j in range(0, dff, ff_chunk):
        u = jnp.dot(hb, wup_ref[:, j:j + ff_chunk], preferred_element_type=F32)
        a = jnp.square(jnp.maximum(u, 0.0)).astype(BF)
        o_ref[...] += jnp.dot(a, wdown_ref[j:j + ff_chunk, :], preferred_element_type=F32)
    if final:
        o_ref[...] = _rms(o_ref[...], fnw_ref[...])


def _outproj_mlp(parts, x2, wout, nw, wup, wdown, fnw=None, tm=512):
    M, D = x2.shape
    dff = wup.shape[1]
    final = fnw is not None
    in_specs = [pl.BlockSpec((tm, p.shape[1]), lambda i: (i, 0)) for p in parts]
    in_specs += [pl.BlockSpec((tm, D), lambda i: (i, 0)),
                 pl.BlockSpec(wout.shape, lambda i: (0, 0)),
                 pl.BlockSpec((1, D), lambda i: (0, 0)),
                 pl.BlockSpec((D, dff), lambda i: (0, 0)),
                 pl.BlockSpec((dff, D), lambda i: (0, 0))]
    args = list(parts) + [x2, wout, nw.reshape(1, D), wup, wdown]
    if final:
        in_specs.append(pl.BlockSpec((1, D), lambda i: (0, 0)))
        args.append(fnw.reshape(1, D))
    return pl.pallas_call(
        functools.partial(_outproj_mlp_kernel, n_parts=len(parts), final=final, ff_chunk=1024),
        grid=(M // tm,),
        in_specs=in_specs,
        out_specs=pl.BlockSpec((tm, D), lambda i: (i, 0)),
        out_shape=jax.ShapeDtypeStruct((M, D), F32),
        compiler_params=_params("parallel"),
        name="outproj_mlp",
    )(*args)


def _retention_kernel(q_ref, k_ref, v_ref, g_ref, cos_ref, sin_ref, dec_ref, qd_ref, kd_ref,
                      cg_ref, o_ref, *, chunk):
    T, dk = q_ref.shape
    dec = dec_ref[...]
    qd = qd_ref[...]
    kd = kd_ref[...]
    cg = cg_ref[...]
    scale = dk ** -0.5
    half = dk // 2

    def rope(x, cos, sin):
        return x * cos + pltpu.roll(x, half, 1) * sin

    def body(n, S):
        r = pl.ds(pl.multiple_of(n * chunk, chunk), chunk)
        cos = cos_ref[r, :]
        sin = sin_ref[r, :]
        q = rope(q_ref[r, :], cos, sin)
        k = rope(k_ref[r, :], cos, sin) * scale
        v = v_ref[r, :]
        s = _mm_nt(q, k) * dec
        o = _mm(s, v) + _mm(q * qd, S)
        S = cg * S + _mm_tn(k * kd, v)
        o = o * lax.rsqrt(jnp.mean(o * o, axis=-1, keepdims=True) + EPS)
        o_ref[r, :] = o * _silu(g_ref[r, :])
        return S

    lax.fori_loop(0, T // chunk, body, jnp.zeros((dk, dk), F32))


def _retention(p, B, T, chunk=RET_CHUNK):
    H, dk, C = RET_HEADS, HEAD_DIM, chunk
    half = dk // 2
    inv = ROPE_THETA ** (-jnp.arange(half, dtype=F32) / half)
    ang = jnp.arange(T, dtype=F32)[:, None] * inv[None, :]
    cos = jnp.concatenate([jnp.cos(ang), jnp.cos(ang)], axis=-1)
    sin = jnp.concatenate([-jnp.sin(ang), jnp.sin(ang)], axis=-1)
    log_gamma = jnp.log1p(-jnp.exp2(-5.0 - jnp.arange(H, dtype=F32)))
    idx = jnp.arange(C, dtype=F32)
    diff = idx[:, None] - idx[None, :]
    causal = diff >= 0
    dec = jnp.where(causal, jnp.exp(log_gamma[:, None, None] * jnp.where(causal, diff, 0.0)), 0.0)
    qd = jnp.broadcast_to(jnp.exp(log_gamma[:, None] * (idx[None, :] + 1.0))[:, :, None], (H, C, dk))
    kd = jnp.broadcast_to(jnp.exp(log_gamma[:, None] * (C - 1.0 - idx[None, :]))[:, :, None], (H, C, dk))
    cg = jnp.broadcast_to(jnp.exp(log_gamma * C)[:, None, None], (H, 1, dk))

    def col(g):
        return pl.BlockSpec((None, T, dk), lambda b, h: (b, 0, g * H + h))

    tab = pl.BlockSpec((T, dk), lambda b, h: (0, 0))
    return pl.pallas_call(
        functools.partial(_retention_kernel, chunk=C),
        grid=(B, H),
        in_specs=[col(0), col(1), col(2), col(3), tab, tab,
                  pl.BlockSpec((None, C, C), lambda b, h: (h, 0, 0)),
                  pl.BlockSpec((None, C, dk), lambda b, h: (h, 0, 0)),
                  pl.BlockSpec((None, C, dk), lambda b, h: (h, 0, 0)),
                  pl.BlockSpec((None, 1, dk), lambda b, h: (h, 0, 0))],
        out_specs=pl.BlockSpec((None, T, dk), lambda b, h: (b, 0, h)),
        out_shape=jax.ShapeDtypeStruct((B, T, H * dk), F32),
        compiler_params=_params("parallel", "parallel"),
        name="retention",
    )(p, p, p, p, cos, sin, dec, qd, kd, cg)


def _scan_rows(a, b):
    R = a.shape[0]
    row = lax.broadcasted_iota(jnp.int32, a.shape, 0)
    d = 1
    while d < R:
        keep = row >= d
        a_s = jnp.where(keep, pltpu.roll(a, d, 0), 1.0)
        b_s = jnp.where(keep, pltpu.roll(b, d, 0), 0.0)
        b = a * b_s + b
        a = a * a_s
        d *= 2
    return a, b


def _rglru_kernel(x_ref, y_ref, cw_ref, cb_ref, wg_ref, bg_ref, lam_ref, o_ref,
                  xpad_ref, h_ref, *, sub):
    tt, W = x_ref.shape
    t = pl.program_id(1)

    @pl.when(t == 0)
    def _():
        xpad_ref[0:SUBLANES, :] = jnp.zeros((SUBLANES, W), F32)
        h_ref[...] = jnp.zeros(h_ref.shape, F32)

    xpad_ref[SUBLANES:SUBLANES + tt, :] = x_ref[...]
    neg_c_sp = -LRU_C * _softplus(-lam_ref[...])
    cw = cw_ref[...]
    cb = cb_ref[...]
    bg = bg_ref[...]
    first = t == 0
    for r0 in range(0, tt, sub):
        xc = cb + cw[0:1, :] * xpad_ref[r0 + 5:r0 + 5 + sub, :]
        for kk in range(1, CONV_K):
            xc = xc + cw[kk:kk + 1, :] * xpad_ref[r0 + 5 + kk:r0 + 5 + kk + sub, :]
        gates = _mm(xc, wg_ref[...]) + bg
        r = jax.nn.sigmoid(gates[:, :W])
        i = jax.nn.sigmoid(gates[:, W:])
        log_a = neg_c_sp * r
        a = jnp.exp(log_a)
        th = jnp.tanh(log_a)
        mult = jnp.sqrt(-2.0 * th / (1.0 - th))
        if r0 == 0:
            row = lax.broadcasted_iota(jnp.int32, mult.shape, 0)
            mult = jnp.where(jnp.logical_and(first, row == 0), 1.0, mult)
        u = xc * i * mult
        acum, hloc = _scan_rows(a, u)
        h = hloc + acum * h_ref[...]
        h_ref[...] = h[sub - 1:sub, :]
        o_ref[r0:r0 + sub, :] = h * jax.nn.gelu(y_ref[r0:r0 + sub, :])
    xpad_ref[0:SUBLANES, :] = xpad_ref[tt:tt + SUBLANES, :]


def _rglru(p, B, T, cw, cb, wg, bg, lam, tt=1024, sub=256):
    W = cw.shape[1]
    tt = min(tt, T)
    return pl.pallas_call(
        functools.partial(_rglru_kernel, sub=sub),
        grid=(B, T // tt),
        in_specs=[pl.BlockSpec((None, tt, W), lambda b, t: (b, t, 4)),
                  pl.BlockSpec((None, tt, W), lambda b, t: (b, t, 5)),
                  pl.BlockSpec((CONV_K, W), lambda b, t: (0, 0)),
                  pl.BlockSpec((1, W), lambda b, t: (0, 0)),
                  pl.BlockSpec((W, 2 * W), lambda b, t: (0, 0)),
                  pl.BlockSpec((1, 2 * W), lambda b, t: (0, 0)),
                  pl.BlockSpec((1, W), lambda b, t: (0, 0))],
        out_specs=pl.BlockSpec((None, tt, W), lambda b, t: (b, t, 0)),
        out_shape=jax.ShapeDtypeStruct((B, T, W), F32),
        scratch_shapes=[pltpu.VMEM((tt + SUBLANES, W), F32), pltpu.VMEM((1, W), F32)],
        compiler_params=_params("parallel", "arbitrary"),
        name="rglru",
    )(p, p, cw, cb.reshape(1, W), wg, bg.reshape(1, 2 * W), lam.reshape(1, W))


def _unit_lower_inverse(Ls, eye, same_sub, chunk):
    Ld = [jnp.where(same_sub, L, 0.0) for L in Ls]
    Lo = [L - d for L, d in zip(Ls, Ld)]
    P = [eye - d for d in Ld]
    M = Ld
    d = 2
    while d < GDN_SUB:
        M = [_mm(m, m) for m in M]
        P = [_mm(p, eye + m) for p, m in zip(P, M)]
        d *= 2
    Nm = [_mm(p, lo) for p, lo in zip(P, Lo)]
    R = [eye - n for n in Nm]
    M = Nm
    d = 2
    while d < chunk // GDN_SUB:
        M = [_mm(m, m) for m in M]
        R = [_mm(r, eye + m) for r, m in zip(R, M)]
        d *= 2
    return [_mm(r, p) for r, p in zip(R, P)]


def _gdn_kernel(q_ref, k_ref, v_ref, z_ref, cw_ref, a_cn_ref, b_cn_ref, a_nc_ref,
                alog_c_ref, dtb_c_ref, alog_r_ref, dtb_r_ref, nw_ref, o_ref,
                tail_ref, qn_ref, kn_ref, vn_ref, u_ref, wq_ref, qk_ref, kg_ref,
                cd_ref, gcn_ref, bcn_ref, glcn_ref, gnc_ref, s_ref, *, chunk, group,
                heads_per_iter):
    tt, HD = q_ref.shape
    dk = HEAD_DIM
    H = HD // dk
    C, R = chunk, group
    G = tt // R
    CPG = R // C
    t = pl.program_id(1)

    @pl.when(t == 0)
    def _():
        tail_ref[...] = jnp.zeros(tail_ref.shape, F32)
        s_ref[...] = jnp.zeros(s_ref.shape, F32)

    cw = cw_ref[...]
    row_tiles = [(SUBLANES, LANES - SUBLANES)] + [(r0, LANES) for r0 in range(LANES, tt, LANES)]
    for idx, (src, dst, l2, scale) in enumerate(((q_ref, qn_ref, True, dk ** -0.5),
                                                 (k_ref, kn_ref, True, 1.0),
                                                 (v_ref, vn_ref, False, 1.0))):
        cwi = cw[:, idx * HD:(idx + 1) * HD]

        def finish(y, r0, n, dst=dst, l2=l2, scale=scale):
            y = _silu(y)
            for h in range(H):
                yh = y[:, h * dk:(h + 1) * dk]
                if l2:
                    yh = yh * (lax.rsqrt(jnp.sum(yh * yh, axis=-1, keepdims=True) + EPS) * scale)
                dst[h, r0:r0 + n, :] = yh

        head = jnp.concatenate([tail_ref[idx], src[0:SUBLANES, :]], axis=0)
        y = cwi[0:1, :] * head[5:5 + SUBLANES, :]
        for kk in range(1, CONV_K):
            y = y + cwi[kk:kk + 1, :] * head[5 + kk:5 + kk + SUBLANES, :]
        finish(y, 0, SUBLANES)
        for r0, n in row_tiles:
            y = cwi[0:1, :] * src[r0 - 3:r0 - 3 + n, :]
            for kk in range(1, CONV_K):
                y = y + cwi[kk:kk + 1, :] * src[r0 - 3 + kk:r0 - 3 + kk + n, :]
            finish(y, r0, n)
        tail_ref[idx] = src[tt - SUBLANES:tt, :]

    ii = lax.broadcasted_iota(jnp.int32, (R, R), 0)
    jj = lax.broadcasted_iota(jnp.int32, (R, R), 1)
    same_chunk = (ii // C) == (jj // C)
    incl = jnp.logical_and(same_chunk, ii >= jj)
    strict = jnp.logical_and(same_chunk, ii > jj)
    same_sub = (ii // GDN_SUB) == (jj // GDN_SUB)
    eye = (ii == jj).astype(F32)
    tril = incl.astype(F32)
    g_cn = -jnp.exp(alog_c_ref[...]) * _softplus(a_cn_ref[...] + dtb_c_ref[...])
    g_nc = -jnp.exp(alog_r_ref[...]) * _softplus(a_nc_ref[...] + dtb_r_ref[...])
    beta_cn = jax.nn.sigmoid(b_cn_ref[...])
    G_cn = _mm_f32(tril, g_cn)
    GL_cn = _mm_f32(same_chunk.astype(F32), g_cn)
    G_nc = _mm_nt_f32(g_nc, tril)
    for h in range(H):
        gcn_ref[h] = G_cn[:, h * G:(h + 1) * G]
        bcn_ref[h] = beta_cn[:, h * G:(h + 1) * G]
        glcn_ref[h] = GL_cn[:, h * G:(h + 1) * G]
        gnc_ref[h] = G_nc[h * G:(h + 1) * G, :]

    i64 = lax.broadcasted_iota(jnp.int32, (C, C), 0)
    j64 = lax.broadcasted_iota(jnp.int32, (C, C), 1)
    incl64 = i64 >= j64

    def prep(hp, carry):
        units = [(hp * heads_per_iter + hh, g) for hh in range(heads_per_iter) for g in range(G)]
        qs, ks, kbs, xs, dms, egs, gcols, grows, glcols = [], [], [], [], [], [], [], [], []
        for h, g in units:
            rows = slice(g * R, (g + 1) * R)
            q = qn_ref[h, rows, :]
            k = kn_ref[h, rows, :]
            v = vn_ref[h, rows, :]
            gcol = gcn_ref[h][:, g:g + 1]
            bcol = bcn_ref[h][:, g:g + 1]
            grow = gnc_ref[h][g:g + 1, :]
            eg = jnp.exp(gcol)
            kb = k * bcol
            qs.append(q)
            ks.append(k)
            kbs.append(kb)
            xs.append(jnp.concatenate([v * bcol, kb * eg], axis=1))
            dms.append(jnp.where(incl, jnp.exp(jnp.where(incl, gcol - grow, 0.0)), 0.0))
            egs.append(eg)
            gcols.append(gcol)
            grows.append(grow)
            glcols.append(glcn_ref[h][:, g:g + 1])
        Ls = [jnp.where(strict, _mm_nt(kb, k) * dm, 0.0) for kb, k, dm in zip(kbs, ks, dms)]
        tinvs = _unit_lower_inverse(Ls, eye, same_sub, C)
        uws = [_mm(tinv, x) for tinv, x in zip(tinvs, xs)]
        for i, (h, g) in enumerate(units):
            rows = slice(g * R, (g + 1) * R)
            q, k, gcol, glcol = qs[i], ks[i], gcols[i], glcols[i]
            u_ref[h, rows, :] = uws[i][:, :dk]
            qg = (q * egs[i]).astype(BF)
            w = uws[i][:, dk:].astype(BF)
            kg_ref[h, rows, :] = (k * jnp.exp(glcol - gcol)).astype(BF)
            for c in range(CPG):
                rc = slice(c * C, (c + 1) * C)
                ci = g * CPG + c
                wq_ref[h, 2 * ci * C:(2 * ci + 1) * C, :] = w[rc]
                wq_ref[h, (2 * ci + 1) * C:(2 * ci + 2) * C, :] = qg[rc]
                dmc = jnp.where(incl64, jnp.exp(jnp.where(incl64, gcol[rc] - grows[i][:, rc], 0.0)), 0.0)
                qk_ref[h, ci * C:(ci + 1) * C, :] = (_mm_nt(q[rc], k[rc]) * dmc).astype(BF)
                cd_ref[h, ci:ci + 1, :] = jnp.broadcast_to(jnp.exp(glcol[c * C:c * C + 1, :]), (1, dk))
        return carry

    lax.fori_loop(0, H // heads_per_iter, prep, 0)

    nw = nw_ref[...]

    def step(n, carry):
        r = pl.ds(pl.multiple_of(n * C, C), C)
        r2 = pl.ds(pl.multiple_of(n * 2 * C, 2 * C), 2 * C)
        S = [s_ref[h] for h in range(H)]
        wqs = [_mm(wq_ref[h, r2, :], S[h]) for h in range(H)]
        vb = [(u_ref[h, r, :] - wqs[h][:C]).astype(BF) for h in range(H)]
        intra = [_mm(qk_ref[h, r, :], vb[h]) for h in range(H)]
        dS = [_mm_tn(kg_ref[h, r, :], vb[h]) for h in range(H)]
        for h in range(H):
            hl = slice(h * dk, (h + 1) * dk)
            s_ref[h] = S[h] * cd_ref[h, pl.ds(n, 1), :] + dS[h]
            o = wqs[h][C:] + intra[h]
            o = o * lax.rsqrt(jnp.mean(o * o, axis=-1, keepdims=True) + EPS)
            o_ref[r, hl] = o * nw * _silu(z_ref[r, hl])
        return carry

    lax.fori_loop(0, tt // C, step, 0)


def _gdn(p, ab, conv_w, a_log, dt_bias, norm_w, B, T):
    H, dk, C, R = GDN_HEADS, HEAD_DIM, GDN_CHUNK, GDN_GROUP
    HD = H * dk
    tt = min(GDN_TILE, T)
    NT, G = T // tt, tt // R

    def cols(a):
        return a.reshape(B, NT, G, R, H).transpose(0, 1, 3, 4, 2).reshape(B, NT, R, H * G)

    def rows(a):
        return a.reshape(B, NT, G, R, H).transpose(0, 1, 4, 2, 3).reshape(B, NT, H * G, R)

    bl = ab[..., :H]
    al = ab[..., H:2 * H]
    alog_c = jnp.repeat(a_log, G).reshape(1, H * G)
    dtb_c = jnp.repeat(dt_bias, G).reshape(1, H * G)

    def col(g):
        return pl.BlockSpec((None, tt, HD), lambda b, t: (b, t, g))

    cn = pl.BlockSpec((None, None, R, H * G), lambda b, t: (b, t, 0, 0))
    nc = pl.BlockSpec((None, None, H * G, R), lambda b, t: (b, t, 0, 0))
    full = lambda shape: pl.BlockSpec(shape, lambda b, t: (0,) * len(shape))
    hbuf = lambda dt: pltpu.VMEM((H, tt, dk), dt)
    return pl.pallas_call(
        functools.partial(_gdn_kernel, chunk=C, group=R, heads_per_iter=GDN_HEADS_PER_ITER),
        grid=(B, NT),
        in_specs=[col(0), col(1), col(2), col(3), full((CONV_K, 3 * HD)), cn, cn, nc,
                  full((1, H * G)), full((1, H * G)), full((H * G, 1)), full((H * G, 1)),
                  full((1, dk))],
        out_specs=pl.BlockSpec((None, tt, HD), lambda b, t: (b, t, 0)),
        out_shape=jax.ShapeDtypeStruct((B, T, HD), F32),
        scratch_shapes=[pltpu.VMEM((3, SUBLANES, HD), F32),
                        hbuf(F32), hbuf(F32), hbuf(F32),
                        hbuf(F32),
                        pltpu.VMEM((H, 2 * tt, dk), BF),
                        pltpu.VMEM((H, tt, C), BF),
                        hbuf(BF),
                        pltpu.VMEM((H, tt // C, dk), F32),
                        pltpu.VMEM((H, R, G), F32),
                        pltpu.VMEM((H, R, G), F32),
                        pltpu.VMEM((H, R, G), F32),
                        pltpu.VMEM((H, G, R), F32),
                        pltpu.VMEM((H, dk, dk), F32)],
        compiler_params=_params("parallel", "arbitrary"),
        name="gated_delta",
    )(p, p, p, p, conv_w, cols(al), cols(bl), rows(al), alog_c, dtb_c,
      alog_c.reshape(H * G, 1), dtb_c.reshape(H * G, 1), norm_w.reshape(1, dk))


def _block_diag(w):
    n, d, _ = w.shape
    eye = jnp.eye(n, dtype=w.dtype)
    return (eye[:, None, :, None] * w[:, :, None, :]).reshape(n * d, n * d)


def kernel(x, mixer_norm_w, mlp_norm_w, final_norm_w, w_in_even, lru_conv_w, lru_conv_b,
           lru_w_r, lru_b_r, lru_w_i, lru_b_i, lru_lambda, w_out_even, w_in_odd, gdn_conv_w,
           gdn_a_log, gdn_dt_bias, gdn_norm_w, w_out_odd, w_up, w_down):
    B, T, D = x.shape
    M = B * T
    x2 = x.reshape(M, D)

    n_even = w_in_even.shape[2]
    (p0,) = _norm_proj(x2, mixer_norm_w[0], w_in_even[0].astype(BF), (n_even,))
    p0 = p0.reshape(B, T, n_even)
    o_ret = _retention(p0, B, T)
    wg = jnp.concatenate([_block_diag(lru_w_r[0]), _block_diag(lru_w_i[0])], axis=1).astype(BF)
    bg = jnp.concatenate([lru_b_r[0], lru_b_i[0]])
    o_lru = _rglru(p0, B, T, lru_conv_w[0], lru_conv_b[0], wg, bg, lru_lambda[0])
    x2 = _outproj_mlp([o_ret.reshape(M, -1), o_lru.reshape(M, -1)], x2,
                      w_out_even[0].astype(BF), mlp_norm_w[0],
                      w_up[0].astype(BF), w_down[0].astype(BF))

    n_main = 4 * GDN_HEADS * HEAD_DIM
    w1 = w_in_odd[0]
    w1 = jnp.pad(w1, ((0, 0), (0, n_main + LANES - w1.shape[1]))).astype(BF)
    p1, ab = _norm_proj(x2, mixer_norm_w[1], w1, (n_main, LANES))
    o_gdn = _gdn(p1.reshape(B, T, n_main), ab.reshape(B, T, LANES), gdn_conv_w[0],
                 gdn_a_log[0], gdn_dt_bias[0], gdn_norm_w[0], B, T)
    x2 = _outproj_mlp([o_gdn.reshape(M, -1)], x2, w_out_odd[0].astype(BF), mlp_norm_w[1],
                      w_up[1].astype(BF), w_down[1].astype(BF), fnw=final_norm_w)
    return x2.reshape(B, T, D)
```

```python
import functools

import numpy as np
import jax
import jax.numpy as jnp
from jax import lax
from jax.experimental import pallas as pl
from jax.experimental.pallas import tpu as pltpu

EPS = 1e-6
ROPE_THETA = 10000.0
LRU_C = 8.0
RET_HEADS = 4
GDN_HEADS = 8
LRU_BLOCKS = 8
CONV_K = 4
HEAD_DIM = 128
RET_CHUNK = 128
RET_UNROLL = 8
GDN_CHUNK = 64
GDN_SUB = 16
GDN_GROUP = 256
GDN_TILE = 512
GDN_HEADS_PER_ITER = 2
GDN_CONV_ROWS = 128
LANES = 128
SUBLANES = 8
VMEM_LIMIT = 56 * 1024 * 1024

BF = jnp.bfloat16
F32 = jnp.float32


def _mm(a, b):
    return jnp.dot(a.astype(BF), b.astype(BF), preferred_element_type=F32)


def _mm_nt(a, b):
    return lax.dot_general(a.astype(BF), b.astype(BF), (((1,), (1,)), ((), ())),
                           preferred_element_type=F32)


def _mm_tn(a, b):
    return lax.dot_general(a.astype(BF), b.astype(BF), (((0,), (0,)), ((), ())),
                           preferred_element_type=F32)


def _mm_f32(a, b):
    return jnp.dot(a, b, preferred_element_type=F32, precision=lax.Precision.HIGHEST)


def _mm_nt_f32(a, b):
    return lax.dot_general(a, b, (((1,), (1,)), ((), ())), preferred_element_type=F32,
                           precision=lax.Precision.HIGHEST)


def _rms(x, w):
    return x * lax.rsqrt(jnp.mean(x * x, axis=-1, keepdims=True) + EPS) * w


def _silu(x):
    hx = 0.5 * x
    return hx + hx * jnp.tanh(hx)


def _sigmoid(x):
    return 0.5 + 0.5 * jnp.tanh(0.5 * x)


def _softplus(x):
    return jnp.maximum(x, 0.0) + jnp.log(1.0 + jnp.exp(-jnp.abs(x)))


def _params(*sem):
    return pltpu.CompilerParams(dimension_semantics=sem, vmem_limit_bytes=VMEM_LIMIT)


def _norm_proj_kernel(x_ref, nw_ref, w_ref, *out_refs, col_chunk):
    hb = _rms(x_ref[...], nw_ref[...]).astype(BF)
    off = 0
    for o_ref in out_refs:
        n = o_ref.shape[-1]
        for j in range(0, n, col_chunk):
            cw = min(col_chunk, n - j)
            o_ref[:, j:j + cw] = jnp.dot(hb, w_ref[:, off + j:off + j + cw],
                                         preferred_element_type=F32).astype(o_ref.dtype)
        off += n


def _norm_proj(x2, nw, w, splits, tm=512):
    M, D = x2.shape
    Nw = w.shape[1]
    assert sum(splits) == Nw and M % tm == 0
    out_shape = [jax.ShapeDtypeStruct((M, n), F32) for n in splits]
    out_specs = [pl.BlockSpec((tm, n), lambda i: (i, 0)) for n in splits]
    return pl.pallas_call(
        functools.partial(_norm_proj_kernel, col_chunk=512),
        grid=(M // tm,),
        in_specs=[pl.BlockSpec((tm, D), lambda i: (i, 0)),
                  pl.BlockSpec((1, D), lambda i: (0, 0)),
                  pl.BlockSpec((D, Nw), lambda i: (0, 0))],
        out_specs=out_specs,
        out_shape=out_shape,
        compiler_params=_params("parallel"),
        name="norm_proj",
    )(x2, nw.reshape(1, D), w)


def _outproj_mlp_kernel(*refs, n_parts, final, ff_chunk):
    parts = refs[:n_parts]
    x_ref, wout_ref, nw_ref, wup_ref, wdown_ref = refs[n_parts:n_parts + 5]
    fnw_ref = refs[n_parts + 5] if final else None
    o_ref = refs[-1]
    x1 = x_ref[...]
    off = 0
    for p in parts:
        kd = p.shape[-1]
        x1 = x1 + jnp.dot(p[...].astype(BF), wout_ref[off:off + kd, :],
                          preferred_element_type=F32)
        off += kd
    hb = _rms(x1, nw_ref[...]).astype(BF)
    o_ref[...] = x1
    dff = wup_ref.shape[1]
    for j in range(0, dff, ff_chunk):
        u = jnp.dot(hb, wup_ref[:, j:j + ff_chunk], preferred_element_type=F32)
        a = jnp.square(jnp.maximum(u, 0.0)).astype(BF)
        o_ref[...] += jnp.dot(a, wdown_ref[j:j + ff_chunk, :], preferred_element_type=F32)
    if final:
        o_ref[...] = _rms(o_ref[...], fnw_ref[...])


def _outproj_mlp(parts, x2, wout, nw, wup, wdown, fnw=None, tm=512):
    M, D = x2.shape
    dff = wup.shape[1]
    final = fnw is not None
    in_specs = [pl.BlockSpec((tm, p.shape[1]), lambda i: (i, 0)) for p in parts]
    in_specs += [pl.BlockSpec((tm, D), lambda i: (i, 0)),
                 pl.BlockSpec(wout.shape, lambda i: (0, 0)),
                 pl.BlockSpec((1, D), lambda i: (0, 0)),
                 pl.BlockSpec((D, dff), lambda i: (0, 0)),
                 pl.BlockSpec((dff, D), lambda i: (0, 0))]
    args = list(parts) + [x2, wout, nw.reshape(1, D), wup, wdown]
    if final:
        in_specs.append(pl.BlockSpec((1, D), lambda i: (0, 0)))
        args.append(fnw.reshape(1, D))
    return pl.pallas_call(
        functools.partial(_outproj_mlp_kernel, n_parts=len(parts), final=final, ff_chunk=1024),
        grid=(M // tm,),
        in_specs=in_specs,
        out_specs=pl.BlockSpec((tm, D), lambda i: (i, 0)),
        out_shape=jax.ShapeDtypeStruct((M, D), F32),
        compiler_params=_params("parallel"),
        name="outproj_mlp",
    )(*args)


def _retention_kernel(q_ref, k_ref, v_ref, g_ref, cos_ref, sin_ref, dec_ref, qd_ref, kd_ref,
                      cg_ref, o_ref, *, chunk, unroll):
    T, dk = q_ref.shape
    dec = dec_ref[...]
    qd = qd_ref[...]
    kd = kd_ref[...]
    cg = cg_ref[...]
    scale = dk ** -0.5
    half = dk // 2

    def rope(x, cos, sin):
        return x * cos + pltpu.roll(x, half, 1) * sin

    def body(i, S):
        rs = [pl.ds(pl.multiple_of((i * unroll + j) * chunk, chunk), chunk) for j in range(unroll)]
        qs = [rope(q_ref[r, :], cos_ref[r, :], sin_ref[r, :]) for r in rs]
        ks = [rope(k_ref[r, :], cos_ref[r, :], sin_ref[r, :]) * scale for r in rs]
        vs = [v_ref[r, :].astype(BF) for r in rs]
        scores = [(_mm_nt(q, k) * dec).astype(BF) for q, k in zip(qs, ks)]
        kvs = [_mm_tn(k * kd, v) for k, v in zip(ks, vs)]
        intra = [_mm(s, v) for s, v in zip(scores, vs)]
        states = []
        for kv in kvs:
            states.append(S)
            S = cg * S + kv
        inter = [_mm(q * qd, Sj) for q, Sj in zip(qs, states)]
        for r, oi, oc in zip(rs, intra, inter):
            o = oi + oc
            o = o * lax.rsqrt(jnp.mean(o * o, axis=-1, keepdims=True) + EPS)
            o_ref[r, :] = o * _silu(g_ref[r, :])
        return S

    lax.fori_loop(0, T // (chunk * unroll), body, jnp.zeros((dk, dk), F32))


def _retention(p, B, T, chunk=RET_CHUNK):
    H, dk, C = RET_HEADS, HEAD_DIM, chunk
    half = dk // 2
    inv = ROPE_THETA ** (-jnp.arange(half, dtype=F32) / half)
    ang = jnp.arange(T, dtype=F32)[:, None] * inv[None, :]
    cos = jnp.concatenate([jnp.cos(ang), jnp.cos(ang)], axis=-1)
    sin = jnp.concatenate([-jnp.sin(ang), jnp.sin(ang)], axis=-1)
    log_gamma = jnp.log1p(-jnp.exp2(-5.0 - jnp.arange(H, dtype=F32)))
    idx = jnp.arange(C, dtype=F32)
    diff = idx[:, None] - idx[None, :]
    causal = diff >= 0
    dec = jnp.where(causal, jnp.exp(log_gamma[:, None, None] * jnp.where(causal, diff, 0.0)), 0.0)
    qd = jnp.broadcast_to(jnp.exp(log_gamma[:, None] * (idx[None, :] + 1.0))[:, :, None], (H, C, dk))
    kd = jnp.broadcast_to(jnp.exp(log_gamma[:, None] * (C - 1.0 - idx[None, :]))[:, :, None], (H, C, dk))
    cg = jnp.broadcast_to(jnp.exp(log_gamma * C)[:, None, None], (H, 1, dk))

    def col(g):
        return pl.BlockSpec((None, T, dk), lambda b, h: (b, 0, g * H + h))

    tab = pl.BlockSpec((T, dk), lambda b, h: (0, 0))
    return pl.pallas_call(
        functools.partial(_retention_kernel, chunk=C, unroll=min(RET_UNROLL, T // C)),
        grid=(B, H),
        in_specs=[col(0), col(1), col(2), col(3), tab, tab,
                  pl.BlockSpec((None, C, C), lambda b, h: (h, 0, 0)),
                  pl.BlockSpec((None, C, dk), lambda b, h: (h, 0, 0)),
                  pl.BlockSpec((None, C, dk), lambda b, h: (h, 0, 0)),
                  pl.BlockSpec((None, 1, dk), lambda b, h: (h, 0, 0))],
        out_specs=pl.BlockSpec((None, T, dk), lambda b, h: (b, 0, h)),
        out_shape=jax.ShapeDtypeStruct((B, T, H * dk), F32),
        compiler_params=_params("parallel", "parallel"),
        name="retention",
    )(p, p, p, p, cos, sin, dec, qd, kd, cg)


def _scan_sublane_groups(a, b):
    pos = lax.broadcasted_iota(jnp.int32, a.shape, 0) % SUBLANES
    d = 1
    while d < SUBLANES:
        keep = pos >= d
        a_s = jnp.where(keep, pltpu.roll(a, d, 0), 1.0)
        b_s = jnp.where(keep, pltpu.roll(b, d, 0), 0.0)
        b = a * b_s + b
        a = a * a_s
        d *= 2
    return a, b


def _rglru_kernel(x_ref, y_ref, cw_ref, cb_ref, wg_ref, bg_ref, lam_ref, o_ref,
                  xpad_ref, h_ref, xc_ref, gate_ref, *, sub, piece):
    tt, W = x_ref.shape
    t = pl.program_id(1)

    @pl.when(t == 0)
    def _():
        xpad_ref[0:SUBLANES, :] = jnp.zeros((SUBLANES, W), F32)
        h_ref[...] = jnp.zeros(h_ref.shape, F32)

    xpad_ref[SUBLANES:SUBLANES + tt, :] = x_ref[...]
    neg_c_sp = -LRU_C * _softplus(-lam_ref[...])
    cw = cw_ref[...]
    cb = cb_ref[...]
    bg = bg_ref[...]
    first = t == 0
    for r0 in range(0, tt, sub):
        for p0 in range(0, sub, piece):
            s = r0 + p0
            xc = cb + cw[0:1, :] * xpad_ref[s + 5:s + 5 + piece, :]
            for kk in range(1, CONV_K):
                xc = xc + cw[kk:kk + 1, :] * xpad_ref[s + 5 + kk:s + 5 + kk + piece, :]
            xc_ref[p0:p0 + piece, :] = xc
        gate_ref[...] = _mm(xc_ref[...], wg_ref[...])
        carry = h_ref[...]
        for p0 in range(0, sub, piece):
            rows = slice(p0, p0 + piece)
            xc = xc_ref[rows, :]
            r = _sigmoid(gate_ref[rows, 0:W] + bg[:, 0:W])
            i = _sigmoid(gate_ref[rows, W:2 * W] + bg[:, W:2 * W])
            a = jnp.exp(neg_c_sp * r)
            m2 = (1.0 - a) * (1.0 + a)
            mult = m2 * lax.rsqrt(jnp.maximum(m2, 1e-30))
            if r0 == 0 and p0 == 0:
                row = lax.broadcasted_iota(jnp.int32, mult.shape, 0)
                mult = jnp.where(jnp.logical_and(first, row == 0), 1.0, mult)
            acum, hloc = _scan_sublane_groups(a, xc * i * mult)
            gy = jax.nn.gelu(y_ref[r0 + p0:r0 + p0 + piece, :])
            for s0 in range(0, piece, SUBLANES):
                g8 = slice(s0, s0 + SUBLANES)
                h = hloc[g8] + acum[g8] * carry
                carry = h[SUBLANES - 1:SUBLANES, :]
                o_ref[r0 + p0 + s0:r0 + p0 + s0 + SUBLANES, :] = h * gy[g8]
        h_ref[...] = carry
    xpad_ref[0:SUBLANES, :] = xpad_ref[tt:tt + SUBLANES, :]


def _rglru(p, B, T, cw, cb, wg, bg, lam, tt=1024, sub=256):
    W = cw.shape[1]
    tt = min(tt, T)
    return pl.pallas_call(
        functools.partial(_rglru_kernel, sub=sub, piece=32),
        grid=(B, T // tt),
        in_specs=[pl.BlockSpec((None, tt, W), lambda b, t: (b, t, 4)),
                  pl.BlockSpec((None, tt, W), lambda b, t: (b, t, 5)),
                  pl.BlockSpec((CONV_K, W), lambda b, t: (0, 0)),
                  pl.BlockSpec((1, W), lambda b, t: (0, 0)),
                  pl.BlockSpec((W, 2 * W), lambda b, t: (0, 0)),
                  pl.BlockSpec((1, 2 * W), lambda b, t: (0, 0)),
                  pl.BlockSpec((1, W), lambda b, t: (0, 0))],
        out_specs=pl.BlockSpec((None, tt, W), lambda b, t: (b, t, 0)),
        out_shape=jax.ShapeDtypeStruct((B, T, W), F32),
        scratch_shapes=[pltpu.VMEM((tt + SUBLANES, W), F32), pltpu.VMEM((1, W), F32),
                        pltpu.VMEM((sub, W), F32), pltpu.VMEM((sub, 2 * W), F32)],
        compiler_params=_params("parallel", "arbitrary"),
        name="rglru",
    )(p, p, cw, cb.reshape(1, W), wg, bg.reshape(1, 2 * W), lam.reshape(1, W))


def _unit_lower_inverse(Ls, eye, same_sub, chunk):
    Ld = [jnp.where(same_sub, L, 0.0) for L in Ls]
    Lo = [L - d for L, d in zip(Ls, Ld)]
    P = [eye - d for d in Ld]
    M = Ld
    d = 2
    while d < GDN_SUB:
        M = [_mm(m, m) for m in M]
        P = [_mm(p, eye + m) for p, m in zip(P, M)]
        d *= 2
    Nm = [_mm(p, lo) for p, lo in zip(P, Lo)]
    R = [eye - n for n in Nm]
    M = Nm
    d = 2
    while d < chunk // GDN_SUB:
        M = [_mm(m, m) for m in M]
        R = [_mm(r, eye + m) for r, m in zip(R, M)]
        d *= 2
    return [_mm(r, p) for r, p in zip(R, P)]


def _gdn_kernel(q_ref, k_ref, v_ref, z_ref, cw_ref, a_cn_ref, b_cn_ref, a_nc_ref,
                alog_c_ref, dtb_c_ref, alog_r_ref, dtb_r_ref, nw_ref, o_ref,
                tail_ref, qn_ref, kn_ref, vn_ref, u_ref, wq_ref, qk_ref, kg_ref,
                cd_ref, gcn_ref, bcn_ref, glcn_ref, gnc_ref, s_ref, *, chunk, group,
                heads_per_iter, conv_rows):
    tt, HD = q_ref.shape
    dk = HEAD_DIM
    H = HD // dk
    C, R = chunk, group
    G = tt // R
    CPG = R // C
    t = pl.program_id(1)

    @pl.when(t == 0)
    def _():
        tail_ref[...] = jnp.zeros(tail_ref.shape, F32)
        s_ref[...] = jnp.zeros(s_ref.shape, F32)

    cw = cw_ref[...]
    streams = ((q_ref, qn_ref, True, dk ** -0.5), (k_ref, kn_ref, True, 1.0),
               (v_ref, vn_ref, False, 1.0))

    def conv_head(h):
        cols = slice(h * dk, (h + 1) * dk)
        for idx, (src, dst, l2, scale) in enumerate(streams):
            cwh = cw[:, idx * HD + h * dk:idx * HD + (h + 1) * dk]
            for r0 in range(0, tt, conv_rows):
                n = conv_rows
                if r0 == 0:
                    win = jnp.concatenate([tail_ref[idx][:, cols], src[0:n, cols]], axis=0)
                else:
                    win = src[r0 - SUBLANES:r0 + n, cols]
                y = cwh[0:1, :] * win[5:5 + n, :]
                for kk in range(1, CONV_K):
                    y = y + cwh[kk:kk + 1, :] * win[5 + kk:5 + kk + n, :]
                y = _silu(y)
                if l2:
                    y = y * (lax.rsqrt(jnp.sum(y * y, axis=-1, keepdims=True) + EPS) * scale)
                dst[h, r0:r0 + n, :] = y

    ii = lax.broadcasted_iota(jnp.int32, (R, R), 0)
    jj = lax.broadcasted_iota(jnp.int32, (R, R), 1)
    same_chunk = (ii // C) == (jj // C)
    incl = jnp.logical_and(same_chunk, ii >= jj)
    strict = jnp.logical_and(same_chunk, ii > jj)
    same_sub = (ii // GDN_SUB) == (jj // GDN_SUB)
    eye = (ii == jj).astype(F32)
    tril = incl.astype(F32)
    g_cn = -jnp.exp(alog_c_ref[...]) * _softplus(a_cn_ref[...] + dtb_c_ref[...])
    g_nc = -jnp.exp(alog_r_ref[...]) * _softplus(a_nc_ref[...] + dtb_r_ref[...])
    beta_cn = jax.nn.sigmoid(b_cn_ref[...])
    G_cn = _mm_f32(tril, g_cn)
    GL_cn = _mm_f32(same_chunk.astype(F32), g_cn)
    G_nc = _mm_nt_f32(g_nc, tril)
    for h in range(H):
        gcn_ref[h] = G_cn[:, h * G:(h + 1) * G]
        bcn_ref[h] = beta_cn[:, h * G:(h + 1) * G]
        glcn_ref[h] = GL_cn[:, h * G:(h + 1) * G]
        gnc_ref[h] = G_nc[h * G:(h + 1) * G, :]

    i64 = lax.broadcasted_iota(jnp.int32, (C, C), 0)
    j64 = lax.broadcasted_iota(jnp.int32, (C, C), 1)
    incl64 = i64 >= j64

    n_pairs = H // heads_per_iter
    for hh in range(heads_per_iter):
        conv_head(hh)

    for hp in range(n_pairs):
        units = [(hp * heads_per_iter + hh, g) for hh in range(heads_per_iter) for g in range(G)]
        qs, ks, kbs, xs, dms, egs, gcols, grows, glcols = [], [], [], [], [], [], [], [], []
        for h, g in units:
            rows = slice(g * R, (g + 1) * R)
            q = qn_ref[h, rows, :]
            k = kn_ref[h, rows, :]
            v = vn_ref[h, rows, :]
            gcol = gcn_ref[h][:, g:g + 1]
            bcol = bcn_ref[h][:, g:g + 1]
            grow = gnc_ref[h][g:g + 1, :]
            eg = jnp.exp(gcol)
            kb = k * bcol
            qs.append(q)
            ks.append(k)
            kbs.append(kb)
            xs.append(jnp.concatenate([v * bcol, kb * eg], axis=1))
            dms.append(jnp.where(incl, jnp.exp(jnp.where(incl, gcol - grow, 0.0)), 0.0))
            egs.append(eg)
            gcols.append(gcol)
            grows.append(grow)
            glcols.append(glcn_ref[h][:, g:g + 1])
        Ls = [jnp.where(strict, _mm_nt(kb, k) * dm, 0.0) for kb, k, dm in zip(kbs, ks, dms)]
        if hp + 1 < n_pairs:
            for hh in range(heads_per_iter):
                conv_head((hp + 1) * heads_per_iter + hh)
        tinvs = _unit_lower_inverse(Ls, eye, same_sub, C)
        uws = [_mm(tinv, x) for tinv, x in zip(tinvs, xs)]
        for i, (h, g) in enumerate(units):
            rows = slice(g * R, (g + 1) * R)
            q, k, gcol, glcol = qs[i], ks[i], gcols[i], glcols[i]
            u_ref[h, rows, :] = uws[i][:, :dk]
            qg = (q * egs[i]).astype(BF)
            w = uws[i][:, dk:].astype(BF)
            kg_ref[h, rows, :] = (k * jnp.exp(glcol - gcol)).astype(BF)
            for c in range(CPG):
                rc = slice(c * C, (c + 1) * C)
                ci = g * CPG + c
                wq_ref[h, 2 * ci * C:(2 * ci + 1) * C, :] = w[rc]
                wq_ref[h, (2 * ci + 1) * C:(2 * ci + 2) * C, :] = qg[rc]
                dmc = jnp.where(incl64, jnp.exp(jnp.where(incl64, gcol[rc] - grows[i][:, rc], 0.0)), 0.0)
                qk_ref[h, ci * C:(ci + 1) * C, :] = (_mm_nt(q[rc], k[rc]) * dmc).astype(BF)
                cd_ref[h, ci:ci + 1, :] = jnp.broadcast_to(jnp.exp(glcol[c * C:c * C + 1, :]), (1, dk))

    for idx, (src, _, _, _) in enumerate(streams):
        tail_ref[idx] = src[tt - SUBLANES:tt, :]

    nw = nw_ref[...]

    def step(n, carry):
        r = pl.ds(pl.multiple_of(n * C, C), C)
        r2 = pl.ds(pl.multiple_of(n * 2 * C, 2 * C), 2 * C)
        S = [s_ref[h] for h in range(H)]
        wqs = [_mm(wq_ref[h, r2, :], S[h]) for h in range(H)]
        vb = [(u_ref[h, r, :] - wqs[h][:C]).astype(BF) for h in range(H)]
        intra = [_mm(qk_ref[h, r, :], vb[h]) for h in range(H)]
        dS = [_mm_tn(kg_ref[h, r, :], vb[h]) for h in range(H)]
        for h in range(H):
            hl = slice(h * dk, (h + 1) * dk)
            s_ref[h] = S[h] * cd_ref[h, pl.ds(n, 1), :] + dS[h]
            o = wqs[h][C:] + intra[h]
            o = o * lax.rsqrt(jnp.mean(o * o, axis=-1, keepdims=True) + EPS)
            o_ref[r, hl] = o * nw * _silu(z_ref[r, hl])
        return carry

    lax.fori_loop(0, tt // C, step, 0)


def _gdn(p, ab, conv_w, a_log, dt_bias, norm_w, B, T):
    H, dk, C, R = GDN_HEADS, HEAD_DIM, GDN_CHUNK, GDN_GROUP
    HD = H * dk
    tt = min(GDN_TILE, T)
    NT, G = T // tt, tt // R

    def cols(a):
        return a.reshape(B, NT, G, R, H).transpose(0, 1, 3, 4, 2).reshape(B, NT, R, H * G)

    def rows(a):
        return a.reshape(B, NT, G, R, H).transpose(0, 1, 4, 2, 3).reshape(B, NT, H * G, R)

    bl = ab[..., :H]
    al = ab[..., H:2 * H]
    alog_c = jnp.repeat(a_log, G).reshape(1, H * G)
    dtb_c = jnp.repeat(dt_bias, G).reshape(1, H * G)

    def col(g):
        return pl.BlockSpec((None, tt, HD), lambda b, t: (b, t, g))

    cn = pl.BlockSpec((None, None, R, H * G), lambda b, t: (b, t, 0, 0))
    nc = pl.BlockSpec((None, None, H * G, R), lambda b, t: (b, t, 0, 0))
    full = lambda shape: pl.BlockSpec(shape, lambda b, t: (0,) * len(shape))
    hbuf = lambda dt: pltpu.VMEM((H, tt, dk), dt)
    return pl.pallas_call(
        functools.partial(_gdn_kernel, chunk=C, group=R, heads_per_iter=GDN_HEADS_PER_ITER,
                          conv_rows=min(GDN_CONV_ROWS, tt)),
        grid=(B, NT),
        in_specs=[col(0), col(1), col(2), col(3), full((CONV_K, 3 * HD)), cn, cn, nc,
                  full((1, H * G)), full((1, H * G)), full((H * G, 1)), full((H * G, 1)),
                  full((1, dk))],
        out_specs=pl.BlockSpec((None, tt, HD), lambda b, t: (b, t, 0)),
        out_shape=jax.ShapeDtypeStruct((B, T, HD), F32),
        scratch_shapes=[pltpu.VMEM((3, SUBLANES, HD), F32),
                        hbuf(F32), hbuf(F32), hbuf(F32),
                        hbuf(F32),
                        pltpu.VMEM((H, 2 * tt, dk), BF),
                        pltpu.VMEM((H, tt, C), BF),
                        hbuf(BF),
                        pltpu.VMEM((H, tt // C, dk), F32),
                        pltpu.VMEM((H, R, G), F32),
                        pltpu.VMEM((H, R, G), F32),
                        pltpu.VMEM((H, R, G), F32),
                        pltpu.VMEM((H, G, R), F32),
                        pltpu.VMEM((H, dk, dk), F32)],
        compiler_params=_params("parallel", "arbitrary"),
        name="gated_delta",
    )(p, p, p, p, conv_w, cols(al), cols(bl), rows(al), alog_c, dtb_c,
      alog_c.reshape(H * G, 1), dtb_c.reshape(H * G, 1), norm_w.reshape(1, dk))


def _block_diag(w):
    n, d, _ = w.shape
    eye = jnp.eye(n, dtype=w.dtype)
    return (eye[:, None, :, None] * w[:, :, None, :]).reshape(n * d, n * d)


def kernel(x, mixer_norm_w, mlp_norm_w, final_norm_w, w_in_even, lru_conv_w, lru_conv_b,
           lru_w_r, lru_b_r, lru_w_i, lru_b_i, lru_lambda, w_out_even, w_in_odd, gdn_conv_w,
           gdn_a_log, gdn_dt_bias, gdn_norm_w, w_out_odd, w_up, w_down):
    B, T, D = x.shape
    M = B * T
    x2 = x.reshape(M, D)

    n_even = w_in_even.shape[2]
    (p0,) = _norm_proj(x2, mixer_norm_w[0], w_in_even[0].astype(BF), (n_even,))
    p0 = p0.reshape(B, T, n_even)
    o_ret = _retention(p0, B, T)
    wg = jnp.concatenate([_block_diag(lru_w_r[0]), _block_diag(lru_w_i[0])], axis=1).astype(BF)
    bg = jnp.concatenate([lru_b_r[0], lru_b_i[0]])
    o_lru = _rglru(p0, B, T, lru_conv_w[0], lru_conv_b[0], wg, bg, lru_lambda[0])
    x2 = _outproj_mlp([o_ret.reshape(M, -1), o_lru.reshape(M, -1)], x2,
                      w_out_even[0].astype(BF), mlp_norm_w[0],
                      w_up[0].astype(BF), w_down[0].astype(BF))

    n_main = 4 * GDN_HEADS * HEAD_DIM
    w1 = w_in_odd[0]
    w1 = jnp.pad(w1, ((0, 0), (0, n_main + LANES - w1.shape[1]))).astype(BF)
    p1, ab = _norm_proj(x2, mixer_norm_w[1], w1, (n_main, LANES))
    o_gdn = _gdn(p1.reshape(B, T, n_main), ab.reshape(B, T, LANES), gdn_conv_w[0],
                 gdn_a_log[0], gdn_dt_bias[0], gdn_norm_w[0], B, T)
    x2 = _outproj_mlp([o_gdn.reshape(M, -1)], x2, w_out_odd[0].astype(BF), mlp_norm_w[1],
                      w_up[1].astype(BF), w_down[1].astype(BF), fnw=final_norm_w)
    return x2.reshape(B, T, D)
```

```python
import functools

import numpy as np
import jax
import jax.numpy as jnp
from jax import lax
from jax.experimental import pallas as pl
from jax.experimental.pallas import tpu as pltpu

EPS = 1e-6
ROPE_THETA = 10000.0
LRU_C = 8.0
RET_HEADS = 4
GDN_HEADS = 8
LRU_BLOCKS = 8
CONV_K = 4
HEAD_DIM = 128
RET_CHUNK = 128
RET_UNROLL = 8
GDN_CHUNK = 64
GDN_SUB = 16
GDN_GROUP = 128
GDN_TILE = 512
GDN_HEADS_PER_ITER = 2
GDN_CONV_ROWS = 128
LANES = 128
SUBLANES = 8
VMEM_LIMIT = 56 * 1024 * 1024

BF = jnp.bfloat16
F32 = jnp.float32


def _mm(a, b):
    return jnp.dot(a.astype(BF), b.astype(BF), preferred_element_type=F32)


def _mm_nt(a, b):
    return lax.dot_general(a.astype(BF), b.astype(BF), (((1,), (1,)), ((), ())),
                           preferred_element_type=F32)


def _mm_tn(a, b):
    return lax.dot_general(a.astype(BF), b.astype(BF), (((0,), (0,)), ((), ())),
                           preferred_element_type=F32)


def _mm_f32(a, b):
    return jnp.dot(a, b, preferred_element_type=F32, precision=lax.Precision.HIGHEST)


def _mm_nt_f32(a, b):
    return lax.dot_general(a, b, (((1,), (1,)), ((), ())), preferred_element_type=F32,
                           precision=lax.Precision.HIGHEST)


def _rms(x, w):
    return x * lax.rsqrt(jnp.mean(x * x, axis=-1, keepdims=True) + EPS) * w


def _silu(x):
    hx = 0.5 * x
    return hx + hx * jnp.tanh(hx)


def _sigmoid(x):
    return 0.5 + 0.5 * jnp.tanh(0.5 * x)


def _softplus(x):
    return jnp.maximum(x, 0.0) + jnp.log(1.0 + jnp.exp(-jnp.abs(x)))


def _params(*sem):
    return pltpu.CompilerParams(dimension_semantics=sem, vmem_limit_bytes=VMEM_LIMIT)


def _norm_proj_kernel(x_ref, nw_ref, w_ref, *out_refs, col_chunk):
    hb = _rms(x_ref[...], nw_ref[...]).astype(BF)
    off = 0
    for o_ref in out_refs:
        n = o_ref.shape[-1]
        for j in range(0, n, col_chunk):
            cw = min(col_chunk, n - j)
            o_ref[:, j:j + cw] = jnp.dot(hb, w_ref[:, off + j:off + j + cw],
                                         preferred_element_type=F32).astype(o_ref.dtype)
        off += n


def _norm_proj(x2, nw, w, splits, tm=512):
    M, D = x2.shape
    Nw = w.shape[1]
    assert sum(splits) == Nw and M % tm == 0
    out_shape = [jax.ShapeDtypeStruct((M, n), F32) for n in splits]
    out_specs = [pl.BlockSpec((tm, n), lambda i: (i, 0)) for n in splits]
    return pl.pallas_call(
        functools.partial(_norm_proj_kernel, col_chunk=512),
        grid=(M // tm,),
        in_specs=[pl.BlockSpec((tm, D), lambda i: (i, 0)),
                  pl.BlockSpec((1, D), lambda i: (0, 0)),
                  pl.BlockSpec((D, Nw), lambda i: (0, 0))],
        out_specs=out_specs,
        out_shape=out_shape,
        compiler_params=_params("parallel"),
        name="norm_proj",
    )(x2, nw.reshape(1, D), w)


def _proj_conv_kernel(x_ref, nw_ref, w_ref, cw_ref, q_out, k_out, v_out, z_out, ab_out,
                      raw_ref, tail_ref, *, tiles_per_seq, col_chunk, conv_rows):
    tm, HD = z_out.shape
    dk = HEAD_DIM
    i = pl.program_id(0)

    @pl.when(i % tiles_per_seq == 0)
    def _():
        tail_ref[...] = jnp.zeros(tail_ref.shape, F32)

    hb = _rms(x_ref[...], nw_ref[...]).astype(BF)
    cw = cw_ref[...]
    chunk_no = 0
    for idx, (dst, l2, scale) in enumerate(((q_out, True, dk ** -0.5), (k_out, True, 1.0),
                                            (v_out, False, 1.0))):
        for c0 in range(0, HD, col_chunk):
            wc = idx * HD + c0
            raw = raw_ref.at[chunk_no % 2]
            chunk_no += 1
            raw[0:SUBLANES, :] = tail_ref[idx, :, c0:c0 + col_chunk]
            raw[SUBLANES:SUBLANES + tm, :] = jnp.dot(hb, w_ref[:, wc:wc + col_chunk],
                                                     preferred_element_type=F32)
            tail_ref[idx, :, c0:c0 + col_chunk] = raw[tm:tm + SUBLANES, :]
            for r0 in range(0, tm, conv_rows):
                n = conv_rows
                for hh in range(col_chunk // dk):
                    cols = slice(hh * dk, (hh + 1) * dk)
                    cwh = cw[:, wc + hh * dk:wc + (hh + 1) * dk]
                    y = cwh[0:1, :] * raw[r0 + 5:r0 + 5 + n, cols]
                    for kk in range(1, CONV_K):
                        y = y + cwh[kk:kk + 1, :] * raw[r0 + 5 + kk:r0 + 5 + kk + n, cols]
                    y = _silu(y)
                    if l2:
                        y = y * (lax.rsqrt(jnp.sum(y * y, axis=-1, keepdims=True) + EPS) * scale)
                    dst[r0:r0 + n, c0 + hh * dk:c0 + (hh + 1) * dk] = y
    for c0 in range(0, HD, 2 * col_chunk):
        wc = 3 * HD + c0
        z_out[:, c0:c0 + 2 * col_chunk] = jnp.dot(hb, w_ref[:, wc:wc + 2 * col_chunk],
                                                  preferred_element_type=F32)
    ab_out[...] = jnp.dot(hb, w_ref[:, 4 * HD:], preferred_element_type=F32)


def _proj_conv(x2, nw, w, conv_w, T, tm=GDN_TILE):
    M, D = x2.shape
    tm = min(tm, T)
    HD = GDN_HEADS * HEAD_DIM
    Nw = w.shape[1]
    assert Nw == 4 * HD + LANES and T % tm == 0
    wide = pl.BlockSpec((tm, HD), lambda i: (i, 0))
    col_chunk = 2 * HEAD_DIM
    return pl.pallas_call(
        functools.partial(_proj_conv_kernel, tiles_per_seq=T // tm, col_chunk=col_chunk,
                          conv_rows=min(GDN_CONV_ROWS, tm)),
        grid=(M // tm,),
        in_specs=[pl.BlockSpec((tm, D), lambda i: (i, 0)),
                  pl.BlockSpec((1, D), lambda i: (0, 0)),
                  pl.BlockSpec((D, Nw), lambda i: (0, 0)),
                  pl.BlockSpec((CONV_K, 3 * HD), lambda i: (0, 0))],
        out_specs=[wide, wide, wide, wide, pl.BlockSpec((tm, LANES), lambda i: (i, 0))],
        out_shape=[jax.ShapeDtypeStruct((M, HD), F32)] * 4 + [jax.ShapeDtypeStruct((M, LANES), F32)],
        scratch_shapes=[pltpu.VMEM((2, tm + SUBLANES, col_chunk), F32),
                        pltpu.VMEM((3, SUBLANES, HD), F32)],
        compiler_params=_params("arbitrary"),
        name="proj_conv",
    )(x2, nw.reshape(1, D), w, conv_w)


def _outproj_mlp_kernel(*refs, n_parts, final, ff_chunk):
    parts = refs[:n_parts]
    x_ref, wout_ref, nw_ref, wup_ref, wdown_ref = refs[n_parts:n_parts + 5]
    fnw_ref = refs[n_parts + 5] if final else None
    o_ref = refs[-1]
    x1 = x_ref[...]
    off = 0
    for p in parts:
        kd = p.shape[-1]
        x1 = x1 + jnp.dot(p[...].astype(BF), wout_ref[off:off + kd, :],
                          preferred_element_type=F32)
        off += kd
    hb = _rms(x1, nw_ref[...]).astype(BF)
    o_ref[...] = x1
    dff = wup_ref.shape[1]
    for j in range(0, dff, ff_chunk):
        u = jnp.dot(hb, wup_ref[:, j:j + ff_chunk], preferred_element_type=F32)
        a = jnp.square(jnp.maximum(u, 0.0)).astype(BF)
        o_ref[...] += jnp.dot(a, wdown_ref[j:j + ff_chunk, :], preferred_element_type=F32)
    if final:
        o_ref[...] = _rms(o_ref[...], fnw_ref[...])


def _outproj_mlp(parts, x2, wout, nw, wup, wdown, fnw=None, tm=512):
    M, D = x2.shape
    dff = wup.shape[1]
    final = fnw is not None
    in_specs = [pl.BlockSpec((tm, p.shape[1]), lambda i: (i, 0)) for p in parts]
    in_specs += [pl.BlockSpec((tm, D), lambda i: (i, 0)),
                 pl.BlockSpec(wout.shape, lambda i: (0, 0)),
                 pl.BlockSpec((1, D), lambda i: (0, 0)),
                 pl.BlockSpec((D, dff), lambda i: (0, 0)),
                 pl.BlockSpec((dff, D), lambda i: (0, 0))]
    args = list(parts) + [x2, wout, nw.reshape(1, D), wup, wdown]
    if final:
        in_specs.append(pl.BlockSpec((1, D), lambda i: (0, 0)))
        args.append(fnw.reshape(1, D))
    return pl.pallas_call(
        functools.partial(_outproj_mlp_kernel, n_parts=len(parts), final=final, ff_chunk=1024),
        grid=(M // tm,),
        in_specs=in_specs,
        out_specs=pl.BlockSpec((tm, D), lambda i: (i, 0)),
        out_shape=jax.ShapeDtypeStruct((M, D), F32),
        compiler_params=_params("parallel"),
        name="outproj_mlp",
    )(*args)


def _retention_kernel(q_ref, k_ref, v_ref, g_ref, cos_ref, sin_ref, dec_ref, qd_ref, kd_ref,
                      cg_ref, o_ref, *, chunk, unroll):
    T, dk = q_ref.shape
    dec = dec_ref[...]
    qd = qd_ref[...]
    kd = kd_ref[...]
    cg = cg_ref[...]
    scale = dk ** -0.5
    half = dk // 2

    def rope(x, cos, sin):
        return x * cos + pltpu.roll(x, half, 1) * sin

    def body(i, S):
        rs = [pl.ds(pl.multiple_of((i * unroll + j) * chunk, chunk), chunk) for j in range(unroll)]
        qs = [rope(q_ref[r, :], cos_ref[r, :], sin_ref[r, :]) for r in rs]
        ks = [rope(k_ref[r, :], cos_ref[r, :], sin_ref[r, :]) * scale for r in rs]
        vs = [v_ref[r, :].astype(BF) for r in rs]
        scores = [(_mm_nt(q, k) * dec).astype(BF) for q, k in zip(qs, ks)]
        kvs = [_mm_tn(k * kd, v) for k, v in zip(ks, vs)]
        intra = [_mm(s, v) for s, v in zip(scores, vs)]
        states = []
        for kv in kvs:
            states.append(S)
            S = cg * S + kv
        inter = [_mm(q * qd, Sj) for q, Sj in zip(qs, states)]
        for r, oi, oc in zip(rs, intra, inter):
            o = oi + oc
            o = o * lax.rsqrt(jnp.mean(o * o, axis=-1, keepdims=True) + EPS)
            o_ref[r, :] = o * _silu(g_ref[r, :])
        return S

    lax.fori_loop(0, T // (chunk * unroll), body, jnp.zeros((dk, dk), F32))


def _retention(p, B, T, chunk=RET_CHUNK):
    H, dk, C = RET_HEADS, HEAD_DIM, chunk
    half = dk // 2
    f32 = np.float32
    inv = np.power(f32(ROPE_THETA), -np.arange(half, dtype=f32) / f32(half)).astype(f32)
    ang = (np.arange(T, dtype=f32)[:, None] * inv[None, :]).astype(f32)
    cos_h = np.cos(ang.astype(np.float64)).astype(f32)
    sin_h = np.sin(ang.astype(np.float64)).astype(f32)
    cos = np.concatenate([cos_h, cos_h], axis=-1)
    sin = np.concatenate([-sin_h, sin_h], axis=-1)
    log_gamma = np.log1p(-np.exp2(-5.0 - np.arange(H, dtype=np.float64)))
    idx = np.arange(C, dtype=np.float64)
    diff = idx[:, None] - idx[None, :]
    causal = diff >= 0
    dec = np.where(causal, np.exp(log_gamma[:, None, None] * np.where(causal, diff, 0.0)), 0.0).astype(f32)
    qd = np.broadcast_to(np.exp(log_gamma[:, None] * (idx[None, :] + 1.0))[:, :, None], (H, C, dk)).astype(f32)
    kd = np.broadcast_to(np.exp(log_gamma[:, None] * (C - 1.0 - idx[None, :]))[:, :, None], (H, C, dk)).astype(f32)
    cg = np.broadcast_to(np.exp(log_gamma * C)[:, None, None], (H, 1, dk)).astype(f32)

    def col(g):
        return pl.BlockSpec((None, T, dk), lambda b, h: (b, 0, g * H + h))

    tab = pl.BlockSpec((T, dk), lambda b, h: (0, 0))
    return pl.pallas_call(
        functools.partial(_retention_kernel, chunk=C, unroll=min(RET_UNROLL, T // C)),
        grid=(B, H),
        in_specs=[col(0), col(1), col(2), col(3), tab, tab,
                  pl.BlockSpec((None, C, C), lambda b, h: (h, 0, 0)),
                  pl.BlockSpec((None, C, dk), lambda b, h: (h, 0, 0)),
                  pl.BlockSpec((None, C, dk), lambda b, h: (h, 0, 0)),
                  pl.BlockSpec((None, 1, dk), lambda b, h: (h, 0, 0))],
        out_specs=pl.BlockSpec((None, T, dk), lambda b, h: (b, 0, h)),
        out_shape=jax.ShapeDtypeStruct((B, T, H * dk), F32),
        compiler_params=_params("parallel", "parallel"),
        name="retention",
    )(p, p, p, p, cos, sin, dec, qd, kd, cg)


def _scan_sublane_groups(a, b):
    pos = lax.broadcasted_iota(jnp.int32, a.shape, 0) % SUBLANES
    d = 1
    while d < SUBLANES:
        keep = pos >= d
        a_s = jnp.where(keep, pltpu.roll(a, d, 0), 1.0)
        b_s = jnp.where(keep, pltpu.roll(b, d, 0), 0.0)
        b = a * b_s + b
        a = a * a_s
        d *= 2
    return a, b


def _rglru_kernel(x_ref, y_ref, cw_ref, cb_ref, wg_ref, bg_ref, lam_ref, o_ref,
                  xpad_ref, h_ref, xc_ref, gate_ref, *, sub, piece):
    tt, W = x_ref.shape
    t = pl.program_id(1)

    @pl.when(t == 0)
    def _():
        xpad_ref[0:SUBLANES, :] = jnp.zeros((SUBLANES, W), F32)
        h_ref[...] = jnp.zeros(h_ref.shape, F32)

    xpad_ref[SUBLANES:SUBLANES + tt, :] = x_ref[...]
    neg_c_sp = -LRU_C * _softplus(-lam_ref[...])
    cw = cw_ref[...]
    cb = cb_ref[...]
    bg = bg_ref[...]
    first = t == 0
    for r0 in range(0, tt, sub):
        for p0 in range(0, sub, piece):
            s = r0 + p0
            xc = cb + cw[0:1, :] * xpad_ref[s + 5:s + 5 + piece, :]
            for kk in range(1, CONV_K):
                xc = xc + cw[kk:kk + 1, :] * xpad_ref[s + 5 + kk:s + 5 + kk + piece, :]
            xc_ref[p0:p0 + piece, :] = xc
        gate_ref[...] = _mm(xc_ref[...], wg_ref[...])
        carry = h_ref[...]
        for p0 in range(0, sub, piece):
            rows = slice(p0, p0 + piece)
            xc = xc_ref[rows, :]
            r = _sigmoid(gate_ref[rows, 0:W] + bg[:, 0:W])
            i = _sigmoid(gate_ref[rows, W:2 * W] + bg[:, W:2 * W])
            a = jnp.exp(neg_c_sp * r)
            m2 = (1.0 - a) * (1.0 + a)
            mult = m2 * lax.rsqrt(jnp.maximum(m2, 1e-30))
            if r0 == 0 and p0 == 0:
                row = lax.broadcasted_iota(jnp.int32, mult.shape, 0)
                mult = jnp.where(jnp.logical_and(first, row == 0), 1.0, mult)
            acum, hloc = _scan_sublane_groups(a, xc * i * mult)
            gy = jax.nn.gelu(y_ref[r0 + p0:r0 + p0 + piece, :])
            for s0 in range(0, piece, SUBLANES):
                g8 = slice(s0, s0 + SUBLANES)
                h = hloc[g8] + acum[g8] * carry
                carry = h[SUBLANES - 1:SUBLANES, :]
                o_ref[r0 + p0 + s0:r0 + p0 + s0 + SUBLANES, :] = h * gy[g8]
        h_ref[...] = carry
    xpad_ref[0:SUBLANES, :] = xpad_ref[tt:tt + SUBLANES, :]


def _rglru(p, B, T, cw, cb, wg, bg, lam, tt=1024, sub=256):
    W = cw.shape[1]
    tt = min(tt, T)
    return pl.pallas_call(
        functools.partial(_rglru_kernel, sub=sub, piece=32),
        grid=(B, T // tt),
        in_specs=[pl.BlockSpec((None, tt, W), lambda b, t: (b, t, 4)),
                  pl.BlockSpec((None, tt, W), lambda b, t: (b, t, 5)),
                  pl.BlockSpec((CONV_K, W), lambda b, t: (0, 0)),
                  pl.BlockSpec((1, W), lambda b, t: (0, 0)),
                  pl.BlockSpec((W, 2 * W), lambda b, t: (0, 0)),
                  pl.BlockSpec((1, 2 * W), lambda b, t: (0, 0)),
                  pl.BlockSpec((1, W), lambda b, t: (0, 0))],
        out_specs=pl.BlockSpec((None, tt, W), lambda b, t: (b, t, 0)),
        out_shape=jax.ShapeDtypeStruct((B, T, W), F32),
        scratch_shapes=[pltpu.VMEM((tt + SUBLANES, W), F32), pltpu.VMEM((1, W), F32),
                        pltpu.VMEM((sub, W), F32), pltpu.VMEM((sub, 2 * W), F32)],
        compiler_params=_params("parallel", "arbitrary"),
        name="rglru",
    )(p, p, cw, cb.reshape(1, W), wg, bg.reshape(1, 2 * W), lam.reshape(1, W))


def _unit_lower_inverse(Ls, diag, same_sub, chunk):
    zero = jnp.zeros((), BF)
    one = jnp.ones((), BF)

    def mmb(a, b):
        return jnp.dot(a, b, preferred_element_type=F32).astype(BF)

    Ld = [jnp.where(same_sub, L, zero) for L in Ls]
    Lo = [jnp.where(same_sub, zero, L) for L in Ls]
    P = [jnp.where(diag, one, -d) for d in Ld]
    M = Ld
    d = 2
    while d < GDN_SUB:
        M = [mmb(m, m) for m in M]
        P = [mmb(p, jnp.where(diag, one, m)) for p, m in zip(P, M)]
        d *= 2
    Nm = [mmb(p, lo) for p, lo in zip(P, Lo)]
    R = [jnp.where(diag, one, -n) for n in Nm]
    M = Nm
    d = 2
    while d < chunk // GDN_SUB:
        M = [mmb(m, m) for m in M]
        R = [mmb(r, jnp.where(diag, one, m)) for r, m in zip(R, M)]
        d *= 2
    return [mmb(r, p) for r, p in zip(R, P)]


def _gdn_kernel(qn_ref, kn_ref, vn_ref, z_ref, a_cn_ref, b_cn_ref, a_nc_ref,
                alog_c_ref, dtb_c_ref, alog_r_ref, dtb_r_ref, nw_ref, o_ref,
                u_ref, wq_ref, qk_ref, kg_ref,
                cd_ref, gcn_ref, bcn_ref, glcn_ref, gnc_ref, s_ref, *, chunk, group,
                heads_per_iter):
    tt, HD = qn_ref.shape
    dk = HEAD_DIM
    H = HD // dk
    C, R = chunk, group
    G = tt // R
    CPG = R // C
    t = pl.program_id(1)

    @pl.when(t == 0)
    def _():
        s_ref[...] = jnp.zeros(s_ref.shape, F32)

    ii = lax.broadcasted_iota(jnp.int32, (R, R), 0)
    jj = lax.broadcasted_iota(jnp.int32, (R, R), 1)
    same_chunk = (ii // C) == (jj // C)
    incl = jnp.logical_and(same_chunk, ii >= jj)
    strict = jnp.logical_and(same_chunk, ii > jj)
    same_sub = (ii // GDN_SUB) == (jj // GDN_SUB)
    tril = incl.astype(F32)
    g_cn = -jnp.exp(alog_c_ref[...]) * _softplus(a_cn_ref[...] + dtb_c_ref[...])
    g_nc = -jnp.exp(alog_r_ref[...]) * _softplus(a_nc_ref[...] + dtb_r_ref[...])
    beta_cn = jax.nn.sigmoid(b_cn_ref[...])
    G_cn = _mm_f32(tril, g_cn)
    GL_cn = _mm_f32(same_chunk.astype(F32), g_cn)
    G_nc = _mm_nt_f32(g_nc, tril)
    for h in range(H):
        gcn_ref[h] = G_cn[:, h * G:(h + 1) * G]
        bcn_ref[h] = beta_cn[:, h * G:(h + 1) * G]
        glcn_ref[h] = GL_cn[:, h * G:(h + 1) * G]
        gnc_ref[h] = G_nc[h * G:(h + 1) * G, :]

    i64 = lax.broadcasted_iota(jnp.int32, (C, C), 0)
    j64 = lax.broadcasted_iota(jnp.int32, (C, C), 1)
    incl64 = i64 >= j64

    for hp in range(H // heads_per_iter):
        units = [(hp * heads_per_iter + hh, g) for hh in range(heads_per_iter) for g in range(G)]
        qs, ks, kbs, xs, dms, egs, gcols, grows, glcols = [], [], [], [], [], [], [], [], []
        for h, g in units:
            rows = slice(g * R, (g + 1) * R)
            hcols = slice(h * dk, (h + 1) * dk)
            q = qn_ref[rows, hcols]
            k = kn_ref[rows, hcols]
            v = vn_ref[rows, hcols]
            gcol = gcn_ref[h][:, g:g + 1]
            bcol = bcn_ref[h][:, g:g + 1]
            grow = gnc_ref[h][g:g + 1, :]
            eg = jnp.exp(gcol)
            kb = k * bcol
            qs.append(q)
            ks.append(k)
            kbs.append(kb)
            xs.append(jnp.concatenate([v * bcol, kb * eg], axis=1))
            dms.append(jnp.where(incl, jnp.exp(jnp.where(incl, gcol - grow, 0.0)), 0.0))
            egs.append(eg)
            gcols.append(gcol)
            grows.append(grow)
            glcols.append(glcn_ref[h][:, g:g + 1])
        Ls = [jnp.where(strict, _mm_nt(kb, k) * dm, 0.0).astype(BF)
              for kb, k, dm in zip(kbs, ks, dms)]
        tinvs = _unit_lower_inverse(Ls, ii == jj, same_sub, C)
        uws = [_mm(tinv, x) for tinv, x in zip(tinvs, xs)]
        for i, (h, g) in enumerate(units):
            rows = slice(g * R, (g + 1) * R)
            q, k, gcol, glcol = qs[i], ks[i], gcols[i], glcols[i]
            u_ref[h, rows, :] = uws[i][:, :dk]
            qg = (q * egs[i]).astype(BF)
            w = uws[i][:, dk:].astype(BF)
            kg_ref[h, rows, :] = (k * jnp.exp(glcol - gcol)).astype(BF)
            for c in range(CPG):
                rc = slice(c * C, (c + 1) * C)
                ci = g * CPG + c
                wq_ref[h, 2 * ci * C:(2 * ci + 1) * C, :] = w[rc]
                wq_ref[h, (2 * ci + 1) * C:(2 * ci + 2) * C, :] = qg[rc]
                dmc = jnp.where(incl64, jnp.exp(jnp.where(incl64, gcol[rc] - grows[i][:, rc], 0.0)), 0.0)
                qk_ref[h, ci * C:(ci + 1) * C, :] = (_mm_nt(q[rc], k[rc]) * dmc).astype(BF)
                cd_ref[h, ci:ci + 1, :] = jnp.broadcast_to(jnp.exp(glcol[c * C:c * C + 1, :]), (1, dk))

    nw = nw_ref[...]

    def step(n, carry):
        r = pl.ds(pl.multiple_of(n * C, C), C)
        r2 = pl.ds(pl.multiple_of(n * 2 * C, 2 * C), 2 * C)
        S = [s_ref[h] for h in range(H)]
        wqs = [_mm(wq_ref[h, r2, :], S[h]) for h in range(H)]
        vb = [(u_ref[h, r, :] - wqs[h][:C]).astype(BF) for h in range(H)]
        intra = [_mm(qk_ref[h, r, :], vb[h]) for h in range(H)]
        dS = [_mm_tn(kg_ref[h, r, :], vb[h]) for h in range(H)]
        for h in range(H):
            hl = slice(h * dk, (h + 1) * dk)
            s_ref[h] = S[h] * cd_ref[h, pl.ds(n, 1), :] + dS[h]
            o = wqs[h][C:] + intra[h]
            o = o * lax.rsqrt(jnp.mean(o * o, axis=-1, keepdims=True) + EPS)
            o_ref[r, hl] = o * nw * _silu(z_ref[r, hl])
        return carry

    lax.fori_loop(0, tt // C, step, 0)


def _gdn(qn, kn, vn, z, ab, a_log, dt_bias, norm_w, B, T):
    H, dk, C, R = GDN_HEADS, HEAD_DIM, GDN_CHUNK, GDN_GROUP
    HD = H * dk
    tt = min(GDN_TILE, T)
    NT, G = T // tt, tt // R

    def cols(a):
        return a.reshape(B, NT, G, R, H).transpose(0, 1, 3, 4, 2).reshape(B, NT, R, H * G)

    def rows(a):
        return a.reshape(B, NT, G, R, H).transpose(0, 1, 4, 2, 3).reshape(B, NT, H * G, R)

    bl = ab[..., :H]
    al = ab[..., H:2 * H]
    alog_c = jnp.repeat(a_log, G).reshape(1, H * G)
    dtb_c = jnp.repeat(dt_bias, G).reshape(1, H * G)

    tile = pl.BlockSpec((None, tt, HD), lambda b, t: (b, t, 0))
    cn = pl.BlockSpec((None, None, R, H * G), lambda b, t: (b, t, 0, 0))
    nc = pl.BlockSpec((None, None, H * G, R), lambda b, t: (b, t, 0, 0))
    full = lambda shape: pl.BlockSpec(shape, lambda b, t: (0,) * len(shape))
    hbuf = lambda dt: pltpu.VMEM((H, tt, dk), dt)
    return pl.pallas_call(
        functools.partial(_gdn_kernel, chunk=C, group=R, heads_per_iter=GDN_HEADS_PER_ITER),
        grid=(B, NT),
        in_specs=[tile, tile, tile, tile, cn, cn, nc,
                  full((1, H * G)), full((1, H * G)), full((H * G, 1)), full((H * G, 1)),
                  full((1, dk))],
        out_specs=tile,
        out_shape=jax.ShapeDtypeStruct((B, T, HD), F32),
        scratch_shapes=[hbuf(F32),
                        pltpu.VMEM((H, 2 * tt, dk), BF),
                        pltpu.VMEM((H, tt, C), BF),
                        hbuf(BF),
                        pltpu.VMEM((H, tt // C, dk), F32),
                        pltpu.VMEM((H, R, G), F32),
                        pltpu.VMEM((H, R, G), F32),
                        pltpu.VMEM((H, R, G), F32),
                        pltpu.VMEM((H, G, R), F32),
                        pltpu.VMEM((H, dk, dk), F32)],
        compiler_params=_params("parallel", "arbitrary"),
        name="gated_delta",
    )(qn, kn, vn, z, cols(al), cols(bl), rows(al), alog_c, dtb_c,
      alog_c.reshape(H * G, 1), dtb_c.reshape(H * G, 1), norm_w.reshape(1, dk))


def _block_diag(w):
    n, d, _ = w.shape
    eye = jnp.eye(n, dtype=w.dtype)
    return (eye[:, None, :, None] * w[:, :, None, :]).reshape(n * d, n * d)


def kernel(x, mixer_norm_w, mlp_norm_w, final_norm_w, w_in_even, lru_conv_w, lru_conv_b,
           lru_w_r, lru_b_r, lru_w_i, lru_b_i, lru_lambda, w_out_even, w_in_odd, gdn_conv_w,
           gdn_a_log, gdn_dt_bias, gdn_norm_w, w_out_odd, w_up, w_down):
    B, T, D = x.shape
    M = B * T
    x2 = x.reshape(M, D)

    n_even = w_in_even.shape[2]
    (p0,) = _norm_proj(x2, mixer_norm_w[0], w_in_even[0].astype(BF), (n_even,))
    p0 = p0.reshape(B, T, n_even)
    o_ret = _retention(p0, B, T)
    wg = jnp.concatenate([_block_diag(lru_w_r[0]), _block_diag(lru_w_i[0])], axis=1).astype(BF)
    bg = jnp.concatenate([lru_b_r[0], lru_b_i[0]])
    o_lru = _rglru(p0, B, T, lru_conv_w[0], lru_conv_b[0], wg, bg, lru_lambda[0])
    x2 = _outproj_mlp([o_ret.reshape(M, -1), o_lru.reshape(M, -1)], x2,
                      w_out_even[0].astype(BF), mlp_norm_w[0],
                      w_up[0].astype(BF), w_down[0].astype(BF))

    n_main = 4 * GDN_HEADS * HEAD_DIM
    w1 = w_in_odd[0]
    w1 = jnp.pad(w1, ((0, 0), (0, n_main + LANES - w1.shape[1]))).astype(BF)
    qn, kn, vn, z, ab = _proj_conv(x2, mixer_norm_w[1], w1, gdn_conv_w[0], T)
    seq = lambda a: a.reshape(B, T, a.shape[-1])
    o_gdn = _gdn(seq(qn), seq(kn), seq(vn), seq(z), seq(ab),
                 gdn_a_log[0], gdn_dt_bias[0], gdn_norm_w[0], B, T)
    x2 = _outproj_mlp([o_gdn.reshape(M, -1)], x2, w_out_odd[0].astype(BF), mlp_norm_w[1],
                      w_up[1].astype(BF), w_down[1].astype(BF), fnw=final_norm_w)
    return x2.reshape(B, T, D)
```

```python
import functools

import numpy as np
import jax
import jax.numpy as jnp
from jax import lax
from jax.experimental import pallas as pl
from jax.experimental.pallas import tpu as pltpu

EPS = 1e-6
ROPE_THETA = 10000.0
LRU_C = 8.0
RET_HEADS = 4
GDN_HEADS = 8
LRU_BLOCKS = 8
CONV_K = 4
HEAD_DIM = 128
RET_CHUNK = 128
RET_UNROLL = 8
GDN_CHUNK = 64
GDN_SUB = 16
GDN_GROUP = 128
GDN_TILE = 512
GDN_HEADS_PER_ITER = 2
GDN_CONV_STRIDE = 4
LANES = 128
SUBLANES = 8
VMEM_LIMIT = 56 * 1024 * 1024

BF = jnp.bfloat16
F32 = jnp.float32


def _mm(a, b):
    return jnp.dot(a.astype(BF), b.astype(BF), preferred_element_type=F32)


def _mm_nt(a, b):
    return lax.dot_general(a.astype(BF), b.astype(BF), (((1,), (1,)), ((), ())),
                           preferred_element_type=F32)


def _mm_tn(a, b):
    return lax.dot_general(a.astype(BF), b.astype(BF), (((0,), (0,)), ((), ())),
                           preferred_element_type=F32)


def _mm_f32(a, b):
    return jnp.dot(a, b, preferred_element_type=F32, precision=lax.Precision.HIGHEST)


def _mm_nt_f32(a, b):
    return lax.dot_general(a, b, (((1,), (1,)), ((), ())), preferred_element_type=F32,
                           precision=lax.Precision.HIGHEST)


def _rms(x, w):
    return x * lax.rsqrt(jnp.mean(x * x, axis=-1, keepdims=True) + EPS) * w


def _silu(x):
    hx = 0.5 * x
    return hx + hx * jnp.tanh(hx)


def _sigmoid(x):
    return 0.5 + 0.5 * jnp.tanh(0.5 * x)


def _softplus(x):
    return jnp.maximum(x, 0.0) + jnp.log(1.0 + jnp.exp(-jnp.abs(x)))


def _params(*sem):
    return pltpu.CompilerParams(dimension_semantics=sem, vmem_limit_bytes=VMEM_LIMIT)


def _norm_proj_kernel(x_ref, nw_ref, w_ref, *out_refs, col_chunk):
    hb = _rms(x_ref[...], nw_ref[...]).astype(BF)
    off = 0
    for o_ref in out_refs:
        n = o_ref.shape[-1]
        for j in range(0, n, col_chunk):
            cw = min(col_chunk, n - j)
            o_ref[:, j:j + cw] = jnp.dot(hb, w_ref[:, off + j:off + j + cw],
                                         preferred_element_type=F32).astype(o_ref.dtype)
        off += n


def _norm_proj(x2, nw, w, splits, tm=512):
    M, D = x2.shape
    Nw = w.shape[1]
    assert sum(splits) == Nw and M % tm == 0
    out_shape = [jax.ShapeDtypeStruct((M, n), F32) for n in splits]
    out_specs = [pl.BlockSpec((tm, n), lambda i: (i, 0)) for n in splits]
    return pl.pallas_call(
        functools.partial(_norm_proj_kernel, col_chunk=512),
        grid=(M // tm,),
        in_specs=[pl.BlockSpec((tm, D), lambda i: (i, 0)),
                  pl.BlockSpec((1, D), lambda i: (0, 0)),
                  pl.BlockSpec((D, Nw), lambda i: (0, 0))],
        out_specs=out_specs,
        out_shape=out_shape,
        compiler_params=_params("parallel"),
        name="norm_proj",
    )(x2, nw.reshape(1, D), w)


def _proj_conv_kernel(x_ref, nw_ref, w_ref, cw_ref, q_out, k_out, v_out, z_out, ab_out,
                      raw_ref, tail_ref, *, tiles_per_seq, col_chunk, stride):
    tm, HD = z_out.shape
    dk = HEAD_DIM
    i = pl.program_id(0)

    @pl.when(i % tiles_per_seq == 0)
    def _():
        tail_ref[...] = jnp.zeros(tail_ref.shape, F32)

    hb = _rms(x_ref[...], nw_ref[...]).astype(BF)
    cw = cw_ref[...]
    heads_per_chunk = col_chunk // dk
    block = stride * SUBLANES
    chunk_no = 0
    for idx, (dst, l2, scale) in enumerate(((q_out, True, dk ** -0.5), (k_out, True, 1.0),
                                            (v_out, False, 1.0))):
        for c0 in range(0, HD, col_chunk):
            wc = idx * HD + c0
            res = jnp.dot(hb, w_ref[:, wc:wc + col_chunk], preferred_element_type=F32)
            for hh in range(heads_per_chunk):
                h = c0 // dk + hh
                raw = raw_ref.at[(chunk_no % 2) * heads_per_chunk + hh]
                raw[0:SUBLANES, :] = tail_ref[idx, h]
                raw[SUBLANES:SUBLANES + tm, :] = res[:, hh * dk:(hh + 1) * dk]
                tail_ref[idx, h] = raw[tm:tm + SUBLANES, :]
                taps = [jnp.broadcast_to(cw[kk:kk + 1, wc + hh * dk:wc + (hh + 1) * dk],
                                         (SUBLANES, dk)) for kk in range(CONV_K)]
                for r0 in range(0, tm, block):
                    xs = {m: raw[pl.ds(SUBLANES + r0 + m, SUBLANES, stride=stride), :]
                          for m in range(1 - CONV_K, stride)}
                    for j in range(stride):
                        y = taps[CONV_K - 1] * xs[j]
                        for kk in range(1, CONV_K):
                            y = y + taps[CONV_K - 1 - kk] * xs[j - kk]
                        y = _silu(y)
                        if l2:
                            y = y * (lax.rsqrt(jnp.sum(y * y, axis=-1, keepdims=True) + EPS)
                                     * scale)
                        dst[h, pl.ds(r0 + j, SUBLANES, stride=stride), :] = y
            chunk_no += 1
    for c0 in range(0, HD, 2 * col_chunk):
        wc = 3 * HD + c0
        z_out[:, c0:c0 + 2 * col_chunk] = jnp.dot(hb, w_ref[:, wc:wc + 2 * col_chunk],
                                                  preferred_element_type=F32)
    ab_out[...] = jnp.dot(hb, w_ref[:, 4 * HD:], preferred_element_type=F32)


def _proj_conv(x2, nw, w, conv_w, T, tm=GDN_TILE):
    M, D = x2.shape
    tm = min(tm, T)
    HD = GDN_HEADS * HEAD_DIM
    Nw = w.shape[1]
    assert Nw == 4 * HD + LANES and T % tm == 0
    H, dk = GDN_HEADS, HEAD_DIM
    heads = pl.BlockSpec((H, tm, dk), lambda i: (0, i, 0))
    col_chunk = 2 * dk
    assert tm % (GDN_CONV_STRIDE * SUBLANES) == 0
    return pl.pallas_call(
        functools.partial(_proj_conv_kernel, tiles_per_seq=T // tm, col_chunk=col_chunk,
                          stride=GDN_CONV_STRIDE),
        grid=(M // tm,),
        in_specs=[pl.BlockSpec((tm, D), lambda i: (i, 0)),
                  pl.BlockSpec((1, D), lambda i: (0, 0)),
                  pl.BlockSpec((D, Nw), lambda i: (0, 0)),
                  pl.BlockSpec((CONV_K, 3 * HD), lambda i: (0, 0))],
        out_specs=[heads, heads, heads, pl.BlockSpec((tm, HD), lambda i: (i, 0)),
                   pl.BlockSpec((tm, LANES), lambda i: (i, 0))],
        out_shape=[jax.ShapeDtypeStruct((H, M, dk), F32)] * 3
                  + [jax.ShapeDtypeStruct((M, HD), F32), jax.ShapeDtypeStruct((M, LANES), F32)],
        scratch_shapes=[pltpu.VMEM((2 * (col_chunk // dk), tm + SUBLANES, dk), F32),
                        pltpu.VMEM((3, H, SUBLANES, dk), F32)],
        compiler_params=_params("arbitrary"),
        name="proj_conv",
    )(x2, nw.reshape(1, D), w, conv_w)


def _outproj_mlp_kernel(*refs, n_parts, final, ff_chunk):
    parts = refs[:n_parts]
    x_ref, wout_ref, nw_ref, wup_ref, wdown_ref = refs[n_parts:n_parts + 5]
    fnw_ref = refs[n_parts + 5] if final else None
    o_ref = refs[-1]
    x1 = x_ref[...]
    off = 0
    for p in parts:
        kd = p.shape[-1]
        x1 = x1 + jnp.dot(p[...].astype(BF), wout_ref[off:off + kd, :],
                          preferred_element_type=F32)
        off += kd
    hb = _rms(x1, nw_ref[...]).astype(BF)
    o_ref[...] = x1
    dff = wup_ref.shape[1]
    for j in range(0, dff, ff_chunk):
        u = jnp.dot(hb, wup_ref[:, j:j + ff_chunk], preferred_element_type=F32)
        a = jnp.square(jnp.maximum(u, 0.0)).astype(BF)
        o_ref[...] += jnp.dot(a, wdown_ref[j:j + ff_chunk, :], preferred_element_type=F32)
    if final:
        o_ref[...] = _rms(o_ref[...], fnw_ref[...])


def _outproj_mlp(parts, x2, wout, nw, wup, wdown, fnw=None, tm=512):
    M, D = x2.shape
    dff = wup.shape[1]
    final = fnw is not None
    in_specs = [pl.BlockSpec((tm, p.shape[1]), lambda i: (i, 0)) for p in parts]
    in_specs += [pl.BlockSpec((tm, D), lambda i: (i, 0)),
                 pl.BlockSpec(wout.shape, lambda i: (0, 0)),
                 pl.BlockSpec((1, D), lambda i: (0, 0)),
                 pl.BlockSpec((D, dff), lambda i: (0, 0)),
                 pl.BlockSpec((dff, D), lambda i: (0, 0))]
    args = list(parts) + [x2, wout, nw.reshape(1, D), wup, wdown]
    if final:
        in_specs.append(pl.BlockSpec((1, D), lambda i: (0, 0)))
        args.append(fnw.reshape(1, D))
    return pl.pallas_call(
        functools.partial(_outproj_mlp_kernel, n_parts=len(parts), final=final, ff_chunk=1024),
        grid=(M // tm,),
        in_specs=in_specs,
        out_specs=pl.BlockSpec((tm, D), lambda i: (i, 0)),
        out_shape=jax.ShapeDtypeStruct((M, D), F32),
        compiler_params=_params("parallel"),
        name="outproj_mlp",
    )(*args)


def _retention_kernel(q_ref, k_ref, v_ref, g_ref, cos_ref, sin_ref, dec_ref, qd_ref, kd_ref,
                      cg_ref, o_ref, *, chunk, unroll):
    T, dk = q_ref.shape
    dec = dec_ref[...]
    qd = qd_ref[...]
    kd = kd_ref[...]
    cg = cg_ref[...]
    scale = dk ** -0.5
    half = dk // 2

    def rope(x, cos, sin):
        return x * cos + pltpu.roll(x, half, 1) * sin

    def body(i, S):
        rs = [pl.ds(pl.multiple_of((i * unroll + j) * chunk, chunk), chunk) for j in range(unroll)]
        qs = [rope(q_ref[r, :], cos_ref[r, :], sin_ref[r, :]) for r in rs]
        ks = [rope(k_ref[r, :], cos_ref[r, :], sin_ref[r, :]) * scale for r in rs]
        vs = [v_ref[r, :].astype(BF) for r in rs]
        scores = [(_mm_nt(q, k) * dec).astype(BF) for q, k in zip(qs, ks)]
        kvs = [_mm_tn(k * kd, v) for k, v in zip(ks, vs)]
        intra = [_mm(s, v) for s, v in zip(scores, vs)]
        states = []
        for kv in kvs:
            states.append(S)
            S = cg * S + kv
        inter = [_mm(q * qd, Sj) for q, Sj in zip(qs, states)]
        for r, oi, oc in zip(rs, intra, inter):
            o = oi + oc
            o = o * lax.rsqrt(jnp.mean(o * o, axis=-1, keepdims=True) + EPS)
            o_ref[r, :] = o * _silu(g_ref[r, :])
        return S

    lax.fori_loop(0, T // (chunk * unroll), body, jnp.zeros((dk, dk), F32))


def _retention(p, B, T, chunk=RET_CHUNK):
    H, dk, C = RET_HEADS, HEAD_DIM, chunk
    half = dk // 2
    f32 = np.float32
    inv = np.power(f32(ROPE_THETA), -np.arange(half, dtype=f32) / f32(half)).astype(f32)
    ang = (np.arange(T, dtype=f32)[:, None] * inv[None, :]).astype(f32)
    cos_h = np.cos(ang.astype(np.float64)).astype(f32)
    sin_h = np.sin(ang.astype(np.float64)).astype(f32)
    cos = np.concatenate([cos_h, cos_h], axis=-1)
    sin = np.concatenate([-sin_h, sin_h], axis=-1)
    log_gamma = np.log1p(-np.exp2(-5.0 - np.arange(H, dtype=np.float64)))
    idx = np.arange(C, dtype=np.float64)
    diff = idx[:, None] - idx[None, :]
    causal = diff >= 0
    dec = np.where(causal, np.exp(log_gamma[:, None, None] * np.where(causal, diff, 0.0)), 0.0).astype(f32)
    qd = np.broadcast_to(np.exp(log_gamma[:, None] * (idx[None, :] + 1.0))[:, :, None], (H, C, dk)).astype(f32)
    kd = np.broadcast_to(np.exp(log_gamma[:, None] * (C - 1.0 - idx[None, :]))[:, :, None], (H, C, dk)).astype(f32)
    cg = np.broadcast_to(np.exp(log_gamma * C)[:, None, None], (H, 1, dk)).astype(f32)

    def col(g):
        return pl.BlockSpec((None, T, dk), lambda b, h: (b, 0, g * H + h))

    tab = pl.BlockSpec((T, dk), lambda b, h: (0, 0))
    return pl.pallas_call(
        functools.partial(_retention_kernel, chunk=C, unroll=min(RET_UNROLL, T // C)),
        grid=(B, H),
        in_specs=[col(0), col(1), col(2), col(3), tab, tab,
                  pl.BlockSpec((None, C, C), lambda b, h: (h, 0, 0)),
                  pl.BlockSpec((None, C, dk), lambda b, h: (h, 0, 0)),
                  pl.BlockSpec((None, C, dk), lambda b, h: (h, 0, 0)),
                  pl.BlockSpec((None, 1, dk), lambda b, h: (h, 0, 0))],
        out_specs=pl.BlockSpec((None, T, dk), lambda b, h: (b, 0, h)),
        out_shape=jax.ShapeDtypeStruct((B, T, H * dk), F32),
        compiler_params=_params("parallel", "parallel"),
        name="retention",
    )(p, p, p, p, cos, sin, dec, qd, kd, cg)


def _scan_sublane_groups(a, b):
    pos = lax.broadcasted_iota(jnp.int32, a.shape, 0) % SUBLANES
    d = 1
    while d < SUBLANES:
        keep = pos >= d
        a_s = jnp.where(keep, pltpu.roll(a, d, 0), 1.0)
        b_s = jnp.where(keep, pltpu.roll(b, d, 0), 0.0)
        b = a * b_s + b
        a = a * a_s
        d *= 2
    return a, b


def _rglru_kernel(x_ref, y_ref, cw_ref, cb_ref, wg_ref, bg_ref, lam_ref, o_ref,
                  xpad_ref, h_ref, xc_ref, gate_ref, *, sub, piece):
    tt, W = x_ref.shape
    t = pl.program_id(1)

    @pl.when(t == 0)
    def _():
        xpad_ref[0:SUBLANES, :] = jnp.zeros((SUBLANES, W), F32)
        h_ref[...] = jnp.zeros(h_ref.shape, F32)

    xpad_ref[SUBLANES:SUBLANES + tt, :] = x_ref[...]
    neg_c_sp = -LRU_C * _softplus(-lam_ref[...])
    cw = cw_ref[...]
    cb = cb_ref[...]
    bg = bg_ref[...]
    first = t == 0
    for r0 in range(0, tt, sub):
        for p0 in range(0, sub, piece):
            s = r0 + p0
            xc = cb + cw[0:1, :] * xpad_ref[s + 5:s + 5 + piece, :]
            for kk in range(1, CONV_K):
                xc = xc + cw[kk:kk + 1, :] * xpad_ref[s + 5 + kk:s + 5 + kk + piece, :]
            xc_ref[p0:p0 + piece, :] = xc
        gate_ref[...] = _mm(xc_ref[...], wg_ref[...])
        carry = h_ref[...]
        for p0 in range(0, sub, piece):
            rows = slice(p0, p0 + piece)
            xc = xc_ref[rows, :]
            r = _sigmoid(gate_ref[rows, 0:W] + bg[:, 0:W])
            i = _sigmoid(gate_ref[rows, W:2 * W] + bg[:, W:2 * W])
            a = jnp.exp(neg_c_sp * r)
            m2 = (1.0 - a) * (1.0 + a)
            mult = m2 * lax.rsqrt(jnp.maximum(m2, 1e-30))
            if r0 == 0 and p0 == 0:
                row = lax.broadcasted_iota(jnp.int32, mult.shape, 0)
                mult = jnp.where(jnp.logical_and(first, row == 0), 1.0, mult)
            acum, hloc = _scan_sublane_groups(a, xc * i * mult)
            gy = jax.nn.gelu(y_ref[r0 + p0:r0 + p0 + piece, :])
            for s0 in range(0, piece, SUBLANES):
                g8 = slice(s0, s0 + SUBLANES)
                h = hloc[g8] + acum[g8] * carry
                carry = h[SUBLANES - 1:SUBLANES, :]
                o_ref[r0 + p0 + s0:r0 + p0 + s0 + SUBLANES, :] = h * gy[g8]
        h_ref[...] = carry
    xpad_ref[0:SUBLANES, :] = xpad_ref[tt:tt + SUBLANES, :]


def _rglru(p, B, T, cw, cb, wg, bg, lam, tt=1024, sub=256):
    W = cw.shape[1]
    tt = min(tt, T)
    return pl.pallas_call(
        functools.partial(_rglru_kernel, sub=sub, piece=32),
        grid=(B, T // tt),
        in_specs=[pl.BlockSpec((None, tt, W), lambda b, t: (b, t, 4)),
                  pl.BlockSpec((None, tt, W), lambda b, t: (b, t, 5)),
                  pl.BlockSpec((CONV_K, W), lambda b, t: (0, 0)),
                  pl.BlockSpec((1, W), lambda b, t: (0, 0)),
                  pl.BlockSpec((W, 2 * W), lambda b, t: (0, 0)),
                  pl.BlockSpec((1, 2 * W), lambda b, t: (0, 0)),
                  pl.BlockSpec((1, W), lambda b, t: (0, 0))],
        out_specs=pl.BlockSpec((None, tt, W), lambda b, t: (b, t, 0)),
        out_shape=jax.ShapeDtypeStruct((B, T, W), F32),
        scratch_shapes=[pltpu.VMEM((tt + SUBLANES, W), F32), pltpu.VMEM((1, W), F32),
                        pltpu.VMEM((sub, W), F32), pltpu.VMEM((sub, 2 * W), F32)],
        compiler_params=_params("parallel", "arbitrary"),
        name="rglru",
    )(p, p, cw, cb.reshape(1, W), wg, bg.reshape(1, 2 * W), lam.reshape(1, W))


def _unit_lower_inverse(Ls, diag, same_sub, chunk):
    zero = jnp.zeros((), BF)
    one = jnp.ones((), BF)

    def mmb(a, b):
        return jnp.dot(a, b, preferred_element_type=F32).astype(BF)

    Ld = [jnp.where(same_sub, L, zero) for L in Ls]
    Lo = [jnp.where(same_sub, zero, L) for L in Ls]
    P = [jnp.where(diag, one, -d) for d in Ld]
    M = Ld
    d = 2
    while d < GDN_SUB:
        M = [mmb(m, m) for m in M]
        P = [mmb(p, jnp.where(diag, one, m)) for p, m in zip(P, M)]
        d *= 2
    Nm = [mmb(p, lo) for p, lo in zip(P, Lo)]
    R = [jnp.where(diag, one, -n) for n in Nm]
    M = Nm
    d = 2
    while d < chunk // GDN_SUB:
        M = [mmb(m, m) for m in M]
        R = [mmb(r, jnp.where(diag, one, m)) for r, m in zip(R, M)]
        d *= 2
    return [mmb(r, p) for r, p in zip(R, P)]


def _gdn_kernel(qn_ref, kn_ref, vn_ref, z_ref, a_cn_ref, b_cn_ref, a_nc_ref,
                alog_c_ref, dtb_c_ref, alog_r_ref, dtb_r_ref, nw_ref, o_ref,
                u_ref, wq_ref, qk_ref, kg_ref,
                cd_ref, gcn_ref, bcn_ref, glcn_ref, gnc_ref, s_ref, *, chunk, group,
                heads_per_iter):
    tt, HD = z_ref.shape
    dk = HEAD_DIM
    H = HD // dk
    C, R = chunk, group
    G = tt // R
    CPG = R // C
    t = pl.program_id(1)

    @pl.when(t == 0)
    def _():
        s_ref[...] = jnp.zeros(s_ref.shape, F32)

    ii = lax.broadcasted_iota(jnp.int32, (R, R), 0)
    jj = lax.broadcasted_iota(jnp.int32, (R, R), 1)
    same_chunk = (ii // C) == (jj // C)
    incl = jnp.logical_and(same_chunk, ii >= jj)
    strict = jnp.logical_and(same_chunk, ii > jj)
    same_sub = (ii // GDN_SUB) == (jj // GDN_SUB)
    tril = incl.astype(F32)
    g_cn = -jnp.exp(alog_c_ref[...]) * _softplus(a_cn_ref[...] + dtb_c_ref[...])
    g_nc = -jnp.exp(alog_r_ref[...]) * _softplus(a_nc_ref[...] + dtb_r_ref[...])
    beta_cn = jax.nn.sigmoid(b_cn_ref[...])
    G_cn = _mm_f32(tril, g_cn)
    GL_cn = _mm_f32(same_chunk.astype(F32), g_cn)
    G_nc = _mm_nt_f32(g_nc, tril)
    for h in range(H):
        gcn_ref[h] = G_cn[:, h * G:(h + 1) * G]
        bcn_ref[h] = beta_cn[:, h * G:(h + 1) * G]
        glcn_ref[h] = GL_cn[:, h * G:(h + 1) * G]
        gnc_ref[h] = G_nc[h * G:(h + 1) * G, :]

    i64 = lax.broadcasted_iota(jnp.int32, (C, C), 0)
    j64 = lax.broadcasted_iota(jnp.int32, (C, C), 1)
    incl64 = i64 >= j64

    for hp in range(H // heads_per_iter):
        units = [(hp * heads_per_iter + hh, g) for hh in range(heads_per_iter) for g in range(G)]
        qs, ks, kbs, xs, dms, egs, gcols, grows, glcols = [], [], [], [], [], [], [], [], []
        for h, g in units:
            rows = slice(g * R, (g + 1) * R)
            q = qn_ref[h, rows, :]
            k = kn_ref[h, rows, :]
            v = vn_ref[h, rows, :]
            gcol = jnp.broadcast_to(gcn_ref[h][:, g:g + 1], (R, dk))
            bcol = jnp.broadcast_to(bcn_ref[h][:, g:g + 1], (R, dk))
            grow = gnc_ref[h][g:g + 1, :]
            eg = jnp.exp(gcol)
            kb = k * bcol
            qs.append(q)
            ks.append(k)
            kbs.append(kb)
            xs.append(jnp.concatenate([v * bcol, kb * eg], axis=1))
            dms.append(jnp.where(incl, jnp.exp(jnp.where(incl, gcol - grow, 0.0)), 0.0))
            egs.append(eg)
            gcols.append(gcol)
            grows.append(grow)
            glcols.append(jnp.broadcast_to(glcn_ref[h][:, g:g + 1], (R, dk)))
        Ls = [jnp.where(strict, _mm_nt(kb, k) * dm, 0.0).astype(BF)
              for kb, k, dm in zip(kbs, ks, dms)]
        tinvs = _unit_lower_inverse(Ls, ii == jj, same_sub, C)
        uws = [_mm(tinv, x) for tinv, x in zip(tinvs, xs)]
        for i, (h, g) in enumerate(units):
            rows = slice(g * R, (g + 1) * R)
            q, k, gcol, glcol = qs[i], ks[i], gcols[i], glcols[i]
            u_ref[h, rows, :] = uws[i][:, :dk]
            qg = (q * egs[i]).astype(BF)
            w = uws[i][:, dk:].astype(BF)
            kg_ref[h, rows, :] = (k * jnp.exp(glcol - gcol)).astype(BF)
            for c in range(CPG):
                rc = slice(c * C, (c + 1) * C)
                ci = g * CPG + c
                wq_ref[h, 2 * ci * C:(2 * ci + 1) * C, :] = w[rc]
                wq_ref[h, (2 * ci + 1) * C:(2 * ci + 2) * C, :] = qg[rc]
                dmc = jnp.where(incl64, jnp.exp(jnp.where(incl64, gcol[rc, :C] - grows[i][:, rc], 0.0)), 0.0)
                qk_ref[h, ci * C:(ci + 1) * C, :] = (_mm_nt(q[rc], k[rc]) * dmc).astype(BF)
                cd_ref[h, ci:ci + 1, :] = jnp.exp(glcol[c * C:c * C + 1, :])

    nw = nw_ref[...]

    def step(n, carry):
        r = pl.ds(pl.multiple_of(n * C, C), C)
        r2 = pl.ds(pl.multiple_of(n * 2 * C, 2 * C), 2 * C)
        S = [s_ref[h] for h in range(H)]
        wqs = [_mm(wq_ref[h, r2, :], S[h]) for h in range(H)]
        vb = [(u_ref[h, r, :] - wqs[h][:C]).astype(BF) for h in range(H)]
        intra = [_mm(qk_ref[h, r, :], vb[h]) for h in range(H)]
        dS = [_mm_tn(kg_ref[h, r, :], vb[h]) for h in range(H)]
        for h in range(H):
            hl = slice(h * dk, (h + 1) * dk)
            s_ref[h] = S[h] * cd_ref[h, pl.ds(n, 1), :] + dS[h]
            o = wqs[h][C:] + intra[h]
            o = o * lax.rsqrt(jnp.mean(o * o, axis=-1, keepdims=True) + EPS)
            o_ref[r, hl] = o * nw * _silu(z_ref[r, hl])
        return carry

    lax.fori_loop(0, tt // C, step, 0)


def _gdn(qn, kn, vn, z, ab, a_log, dt_bias, norm_w, B, T):
    H, dk, C, R = GDN_HEADS, HEAD_DIM, GDN_CHUNK, GDN_GROUP
    HD = H * dk
    tt = min(GDN_TILE, T)
    NT, G = T // tt, tt // R

    def cols(a):
        return a.reshape(B, NT, G, R, H).transpose(0, 1, 3, 4, 2).reshape(B, NT, R, H * G)

    def rows(a):
        return a.reshape(B, NT, G, R, H).transpose(0, 1, 4, 2, 3).reshape(B, NT, H * G, R)

    bl = ab[..., :H]
    al = ab[..., H:2 * H]
    alog_c = jnp.repeat(a_log, G).reshape(1, H * G)
    dtb_c = jnp.repeat(dt_bias, G).reshape(1, H * G)

    tile = pl.BlockSpec((None, tt, HD), lambda b, t: (b, t, 0))
    heads = pl.BlockSpec((H, tt, dk), lambda b, t: (0, b * NT + t, 0))
    cn =pl.BlockSpec((None, None, R, H * G), lambda b, t: (b, t, 0, 0))
    nc = pl.BlockSpec((None, None, H * G, R), lambda b, t: (b, t, 0, 0))
    full = lambda shape: pl.BlockSpec(shape, lambda b, t: (0,) * len(shape))
    hbuf = lambda dt: pltpu.VMEM((H, tt, dk), dt)
    return pl.pallas_call(
        functools.partial(_gdn_kernel, chunk=C, group=R, heads_per_iter=GDN_HEADS_PER_ITER),
        grid=(B, NT),
        in_specs=[heads, heads, heads, tile, cn, cn, nc,
                  full((1, H * G)), full((1, H * G)), full((H * G, 1)), full((H * G, 1)),
                  full((1, dk))],
        out_specs=tile,
        out_shape=jax.ShapeDtypeStruct((B, T, HD), F32),
        scratch_shapes=[hbuf(F32),
                        pltpu.VMEM((H, 2 * tt, dk), BF),
                        pltpu.VMEM((H, tt, C), BF),
                        hbuf(BF),
                        pltpu.VMEM((H, tt // C, dk), F32),
                        pltpu.VMEM((H, R, G), F32),
                        pltpu.VMEM((H, R, G), F32),
                        pltpu.VMEM((H, R, G), F32),
                        pltpu.VMEM((H, G, R), F32),
                        pltpu.VMEM((H, dk, dk), F32)],
        compiler_params=_params("parallel", "arbitrary"),
        name="gated_delta",
    )(qn, kn, vn, z, cols(al), cols(bl), rows(al), alog_c, dtb_c,
      alog_c.reshape(H * G, 1), dtb_c.reshape(H * G, 1), norm_w.reshape(1, dk))


def _block_diag(w):
    n, d, _ = w.shape
    eye = jnp.eye(n, dtype=w.dtype)
    return (eye[:, None, :, None] * w[:, :, None, :]).reshape(n * d, n * d)


def kernel(x, mixer_norm_w, mlp_norm_w, final_norm_w, w_in_even, lru_conv_w, lru_conv_b,
           lru_w_r, lru_b_r, lru_w_i, lru_b_i, lru_lambda, w_out_even, w_in_odd, gdn_conv_w,
           gdn_a_log, gdn_dt_bias, gdn_norm_w, w_out_odd, w_up, w_down):
    B, T, D = x.shape
    M = B * T
    x2 = x.reshape(M, D)

    n_even = w_in_even.shape[2]
    (p0,) = _norm_proj(x2, mixer_norm_w[0], w_in_even[0].astype(BF), (n_even,))
    p0 = p0.reshape(B, T, n_even)
    o_ret = _retention(p0, B, T)
    wg = jnp.concatenate([_block_diag(lru_w_r[0]), _block_diag(lru_w_i[0])], axis=1).astype(BF)
    bg = jnp.concatenate([lru_b_r[0], lru_b_i[0]])
    o_lru = _rglru(p0, B, T, lru_conv_w[0], lru_conv_b[0], wg, bg, lru_lambda[0])
    x2 = _outproj_mlp([o_ret.reshape(M, -1), o_lru.reshape(M, -1)], x2,
                      w_out_even[0].astype(BF), mlp_norm_w[0],
                      w_up[0].astype(BF), w_down[0].astype(BF))

    n_main = 4 * GDN_HEADS * HEAD_DIM
    w1 = w_in_odd[0]
    w1 = jnp.pad(w1, ((0, 0), (0, n_main + LANES - w1.shape[1]))).astype(BF)
    qn, kn, vn, z, ab = _proj_conv(x2, mixer_norm_w[1], w1, gdn_conv_w[0], T)
    seq = lambda a: a.reshape(B, T, a.shape[-1])
    o_gdn = _gdn(qn, kn, vn, seq(z), seq(ab),
                 gdn_a_log[0], gdn_dt_bias[0], gdn_norm_w[0], B, T)
    x2 = _outproj_mlp([o_gdn.reshape(M, -1)], x2, w_out_odd[0].astype(BF), mlp_norm_w[1],
                      w_up[1].astype(BF), w_down[1].astype(BF), fnw=final_norm_w)
    return x2.reshape(B, T, D)
```

```python
import functools

import numpy as np
import jax
import jax.numpy as jnp
from jax import lax
from jax.experimental import pallas as pl
from jax.experimental.pallas import tpu as pltpu

EPS = 1e-6
ROPE_THETA = 10000.0
LRU_C = 8.0
RET_HEADS = 4
GDN_HEADS = 8
LRU_BLOCKS = 8
CONV_K = 4
HEAD_DIM = 128
RET_CHUNK = 128
RET_UNROLL = 8
GDN_CHUNK = 64
GDN_SUB = 16
GDN_GROUP = 128
GDN_TILE = 512
GDN_HEADS_PER_ITER = 4
GDN_CONV_STRIDE = 4
LANES = 128
SUBLANES = 8
VMEM_LIMIT = 56 * 1024 * 1024

BF = jnp.bfloat16
F32 = jnp.float32


def _mm(a, b):
    return jnp.dot(a.astype(BF), b.astype(BF), preferred_element_type=F32)


def _mm_nt(a, b):
    return lax.dot_general(a.astype(BF), b.astype(BF), (((1,), (1,)), ((), ())),
                           preferred_element_type=F32)


def _mm_tn(a, b):
    return lax.dot_general(a.astype(BF), b.astype(BF), (((0,), (0,)), ((), ())),
                           preferred_element_type=F32)


def _mm_f32(a, b):
    return jnp.dot(a, b, preferred_element_type=F32, precision=lax.Precision.HIGHEST)


def _mm_nt_f32(a, b):
    return lax.dot_general(a, b, (((1,), (1,)), ((), ())), preferred_element_type=F32,
                           precision=lax.Precision.HIGHEST)


def _rms(x, w):
    return x * lax.rsqrt(jnp.mean(x * x, axis=-1, keepdims=True) + EPS) * w


def _silu(x):
    hx = 0.5 * x
    return hx + hx * jnp.tanh(hx)


def _sigmoid(x):
    return 0.5 + 0.5 * jnp.tanh(0.5 * x)


def _softplus(x):
    return jnp.maximum(x, 0.0) + jnp.log(1.0 + jnp.exp(-jnp.abs(x)))


def _params(*sem):
    return pltpu.CompilerParams(dimension_semantics=sem, vmem_limit_bytes=VMEM_LIMIT)


def _norm_proj_kernel(x_ref, nw_ref, w_ref, *out_refs, col_chunk):
    hb = _rms(x_ref[...], nw_ref[...]).astype(BF)
    off = 0
    for o_ref in out_refs:
        n = o_ref.shape[-1]
        for j in range(0, n, col_chunk):
            cw = min(col_chunk, n - j)
            o_ref[:, j:j + cw] = jnp.dot(hb, w_ref[:, off + j:off + j + cw],
                                         preferred_element_type=F32).astype(o_ref.dtype)
        off += n


def _norm_proj(x2, nw, w, splits, tm=512):
    M, D = x2.shape
    Nw = w.shape[1]
    assert sum(splits) == Nw and M % tm == 0
    out_shape = [jax.ShapeDtypeStruct((M, n), F32) for n in splits]
    out_specs = [pl.BlockSpec((tm, n), lambda i: (i, 0)) for n in splits]
    return pl.pallas_call(
        functools.partial(_norm_proj_kernel, col_chunk=512),
        grid=(M // tm,),
        in_specs=[pl.BlockSpec((tm, D), lambda i: (i, 0)),
                  pl.BlockSpec((1, D), lambda i: (0, 0)),
                  pl.BlockSpec((D, Nw), lambda i: (0, 0))],
        out_specs=out_specs,
        out_shape=out_shape,
        compiler_params=_params("parallel"),
        name="norm_proj",
    )(x2, nw.reshape(1, D), w)


def _proj_conv_kernel(x_ref, nw_ref, w_ref, cw_ref, q_out, k_out, v_out, z_out, ab_out,
                      raw_ref, tail_ref, *, tiles_per_seq, col_chunk, stride):
    tm, HD = z_out.shape
    dk = HEAD_DIM
    i = pl.program_id(0)

    @pl.when(i % tiles_per_seq == 0)
    def _():
        tail_ref[...] = jnp.zeros(tail_ref.shape, F32)

    hb = _rms(x_ref[...], nw_ref[...]).astype(BF)
    cw = cw_ref[...]
    heads_per_chunk = col_chunk // dk
    block = stride * SUBLANES
    chunk_no = 0
    for idx, (dst, l2, scale) in enumerate(((q_out, True, dk ** -0.5), (k_out, True, 1.0),
                                            (v_out, False, 1.0))):
        for c0 in range(0, HD, col_chunk):
            wc = idx * HD + c0
            res = jnp.dot(hb, w_ref[:, wc:wc + col_chunk], preferred_element_type=F32)
            for hh in range(heads_per_chunk):
                h = c0 // dk + hh
                raw = raw_ref.at[(chunk_no % 2) * heads_per_chunk + hh]
                raw[0:SUBLANES, :] = tail_ref[idx, h]
                raw[SUBLANES:SUBLANES + tm, :] = res[:, hh * dk:(hh + 1) * dk]
                tail_ref[idx, h] = raw[tm:tm + SUBLANES, :]
                taps = [jnp.broadcast_to(cw[kk:kk + 1, wc + hh * dk:wc + (hh + 1) * dk],
                                         (SUBLANES, dk)) for kk in range(CONV_K)]
                for r0 in range(0, tm, block):
                    xs = {m: raw[pl.ds(SUBLANES + r0 + m, SUBLANES, stride=stride), :]
                          for m in range(1 - CONV_K, stride)}
                    for j in range(stride):
                        y = taps[CONV_K - 1] * xs[j]
                        for kk in range(1, CONV_K):
                            y = y + taps[CONV_K - 1 - kk] * xs[j - kk]
                        y = _silu(y)
                        if l2:
                            y = y * (lax.rsqrt(jnp.sum(y * y, axis=-1, keepdims=True) + EPS)
                                     * scale)
                        dst[h, pl.ds(r0 + j, SUBLANES, stride=stride), :] = y
            chunk_no += 1
    for c0 in range(0, HD, 2 * col_chunk):
        wc = 3 * HD + c0
        z_out[:, c0:c0 + 2 * col_chunk] = jnp.dot(hb, w_ref[:, wc:wc + 2 * col_chunk],
                                                  preferred_element_type=F32)
    ab_out[...] = jnp.dot(hb, w_ref[:, 4 * HD:], preferred_element_type=F32)


def _proj_conv(x2, nw, w, conv_w, T, tm=GDN_TILE):
    M, D = x2.shape
    tm = min(tm, T)
    HD = GDN_HEADS * HEAD_DIM
    Nw = w.shape[1]
    assert Nw == 4 * HD + LANES and T % tm == 0
    H, dk = GDN_HEADS, HEAD_DIM
    heads = pl.BlockSpec((H, tm, dk), lambda i: (0, i, 0))
    col_chunk = 2 * dk
    assert tm % (GDN_CONV_STRIDE * SUBLANES) == 0
    return pl.pallas_call(
        functools.partial(_proj_conv_kernel, tiles_per_seq=T // tm, col_chunk=col_chunk,
                          stride=GDN_CONV_STRIDE),
        grid=(M // tm,),
        in_specs=[pl.BlockSpec((tm, D), lambda i: (i, 0)),
                  pl.BlockSpec((1, D), lambda i: (0, 0)),
                  pl.BlockSpec((D, Nw), lambda i: (0, 0)),
                  pl.BlockSpec((CONV_K, 3 * HD), lambda i: (0, 0))],
        out_specs=[heads, heads, heads, pl.BlockSpec((tm, HD), lambda i: (i, 0)),
                   pl.BlockSpec((tm, LANES), lambda i: (i, 0))],
        out_shape=[jax.ShapeDtypeStruct((H, M, dk), F32)] * 3
                  + [jax.ShapeDtypeStruct((M, HD), F32), jax.ShapeDtypeStruct((M, LANES), F32)],
        scratch_shapes=[pltpu.VMEM((2 * (col_chunk // dk), tm + SUBLANES, dk), F32),
                        pltpu.VMEM((3, H, SUBLANES, dk), F32)],
        compiler_params=_params("arbitrary"),
        name="proj_conv",
    )(x2, nw.reshape(1, D), w, conv_w)


def _outproj_mlp_kernel(*refs, n_parts, final, ff_chunk):
    parts = refs[:n_parts]
    x_ref, wout_ref, nw_ref, wup_ref, wdown_ref = refs[n_parts:n_parts + 5]
    fnw_ref = refs[n_parts + 5] if final else None
    o_ref = refs[-1]
    x1 = x_ref[...]
    off = 0
    for p in parts:
        kd = p.shape[-1]
        x1 = x1 + jnp.dot(p[...].astype(BF), wout_ref[off:off + kd, :],
                          preferred_element_type=F32)
        off += kd
    hb = _rms(x1, nw_ref[...]).astype(BF)
    o_ref[...] = x1
    dff = wup_ref.shape[1]
    for j in range(0, dff, ff_chunk):
        u = jnp.dot(hb, wup_ref[:, j:j + ff_chunk], preferred_element_type=F32)
        a = jnp.square(jnp.maximum(u, 0.0)).astype(BF)
        o_ref[...] += jnp.dot(a, wdown_ref[j:j + ff_chunk, :], preferred_element_type=F32)
    if final:
        o_ref[...] = _rms(o_ref[...], fnw_ref[...])


def _outproj_mlp(parts, x2, wout, nw, wup, wdown, fnw=None, tm=512):
    M, D = x2.shape
    dff = wup.shape[1]
    final = fnw is not None
    in_specs = [pl.BlockSpec((tm, p.shape[1]), lambda i: (i, 0)) for p in parts]
    in_specs += [pl.BlockSpec((tm, D), lambda i: (i, 0)),
                 pl.BlockSpec(wout.shape, lambda i: (0, 0)),
                 pl.BlockSpec((1, D), lambda i: (0, 0)),
                 pl.BlockSpec((D, dff), lambda i: (0, 0)),
                 pl.BlockSpec((dff, D), lambda i: (0, 0))]
    args = list(parts) + [x2, wout, nw.reshape(1, D), wup, wdown]
    if final:
        in_specs.append(pl.BlockSpec((1, D), lambda i: (0, 0)))
        args.append(fnw.reshape(1, D))
    return pl.pallas_call(
        functools.partial(_outproj_mlp_kernel, n_parts=len(parts), final=final, ff_chunk=1024),
        grid=(M // tm,),
        in_specs=in_specs,
        out_specs=pl.BlockSpec((tm, D), lambda i: (i, 0)),
        out_shape=jax.ShapeDtypeStruct((M, D), F32),
        compiler_params=_params("parallel"),
        name="outproj_mlp",
    )(*args)


def _retention_kernel(q_ref, k_ref, v_ref, g_ref, cos_ref, sin_ref, dec_ref, qd_ref, kd_ref,
                      cg_ref, o_ref, *, chunk, unroll):
    T, dk = q_ref.shape
    dec = dec_ref[...]
    qd = qd_ref[...]
    kd = kd_ref[...]
    cg = cg_ref[...]
    scale = dk ** -0.5
    half = dk // 2

    def rope(x, cos, sin):
        return x * cos + pltpu.roll(x, half, 1) * sin

    def body(i, S):
        rs = [pl.ds(pl.multiple_of((i * unroll + j) * chunk, chunk), chunk) for j in range(unroll)]
        qs = [rope(q_ref[r, :], cos_ref[r, :], sin_ref[r, :]) for r in rs]
        ks = [rope(k_ref[r, :], cos_ref[r, :], sin_ref[r, :]) * scale for r in rs]
        vs = [v_ref[r, :].astype(BF) for r in rs]
        scores = [(_mm_nt(q, k) * dec).astype(BF) for q, k in zip(qs, ks)]
        kvs = [_mm_tn(k * kd, v) for k, v in zip(ks, vs)]
        intra = [_mm(s, v) for s, v in zip(scores, vs)]
        states = []
        for kv in kvs:
            states.append(S)
            S = cg * S + kv
        inter = [_mm(q * qd, Sj) for q, Sj in zip(qs, states)]
        for r, oi, oc in zip(rs, intra, inter):
            o = oi + oc
            o = o * lax.rsqrt(jnp.mean(o * o, axis=-1, keepdims=True) + EPS)
            o_ref[r, :] = o * _silu(g_ref[r, :])
        return S

    lax.fori_loop(0, T // (chunk * unroll), body, jnp.zeros((dk, dk), F32))


def _retention(p, B, T, chunk=RET_CHUNK):
    H, dk, C = RET_HEADS, HEAD_DIM, chunk
    half = dk // 2
    f32 = np.float32
    inv = np.power(f32(ROPE_THETA), -np.arange(half, dtype=f32) / f32(half)).astype(f32)
    ang = (np.arange(T, dtype=f32)[:, None] * inv[None, :]).astype(f32)
    cos_h = np.cos(ang.astype(np.float64)).astype(f32)
    sin_h = np.sin(ang.astype(np.float64)).astype(f32)
    cos = np.concatenate([cos_h, cos_h], axis=-1)
    sin = np.concatenate([-sin_h, sin_h], axis=-1)
    log_gamma = np.log1p(-np.exp2(-5.0 - np.arange(H, dtype=np.float64)))
    idx = np.arange(C, dtype=np.float64)
    diff = idx[:, None] - idx[None, :]
    causal = diff >= 0
    dec = np.where(causal, np.exp(log_gamma[:, None, None] * np.where(causal, diff, 0.0)), 0.0).astype(f32)
    qd = np.broadcast_to(np.exp(log_gamma[:, None] * (idx[None, :] + 1.0))[:, :, None], (H, C, dk)).astype(f32)
    kd = np.broadcast_to(np.exp(log_gamma[:, None] * (C - 1.0 - idx[None, :]))[:, :, None], (H, C, dk)).astype(f32)
    cg = np.broadcast_to(np.exp(log_gamma * C)[:, None, None], (H, 1, dk)).astype(f32)

    def col(g):
        return pl.BlockSpec((None, T, dk), lambda b, h: (b, 0, g * H + h))

    tab = pl.BlockSpec((T, dk), lambda b, h: (0, 0))
    return pl.pallas_call(
        functools.partial(_retention_kernel, chunk=C, unroll=min(RET_UNROLL, T // C)),
        grid=(B, H),
        in_specs=[col(0), col(1), col(2), col(3), tab, tab,
                  pl.BlockSpec((None, C, C), lambda b, h: (h, 0, 0)),
                  pl.BlockSpec((None, C, dk), lambda b, h: (h, 0, 0)),
                  pl.BlockSpec((None, C, dk), lambda b, h: (h, 0, 0)),
                  pl.BlockSpec((None, 1, dk), lambda b, h: (h, 0, 0))],
        out_specs=pl.BlockSpec((None, T, dk), lambda b, h: (b, 0, h)),
        out_shape=jax.ShapeDtypeStruct((B, T, H * dk), F32),
        compiler_params=_params("parallel", "parallel"),
        name="retention",
    )(p, p, p, p, cos, sin, dec, qd, kd, cg)


def _scan_sublane_groups(a, b):
    pos = lax.broadcasted_iota(jnp.int32, a.shape, 0) % SUBLANES
    d = 1
    while d < SUBLANES:
        keep = pos >= d
        a_s = jnp.where(keep, pltpu.roll(a, d, 0), 1.0)
        b_s = jnp.where(keep, pltpu.roll(b, d, 0), 0.0)
        b = a * b_s + b
        a = a * a_s
        d *= 2
    return a, b


def _rglru_kernel(x_ref, y_ref, cw_ref, cb_ref, wg_ref, bg_ref, lam_ref, o_ref,
                  xpad_ref, h_ref, xc_ref, gate_ref, *, sub, piece):
    tt, W = x_ref.shape
    t = pl.program_id(1)

    @pl.when(t == 0)
    def _():
        xpad_ref[0:SUBLANES, :] = jnp.zeros((SUBLANES, W), F32)
        h_ref[...] = jnp.zeros(h_ref.shape, F32)

    xpad_ref[SUBLANES:SUBLANES + tt, :] = x_ref[...]
    neg_c_sp = -LRU_C * _softplus(-lam_ref[...])
    cw = cw_ref[...]
    cb = cb_ref[...]
    bg = bg_ref[...]
    first = t == 0
    for r0 in range(0, tt, sub):
        for p0 in range(0, sub, piece):
            s = r0 + p0
            xc = cb + cw[0:1, :] * xpad_ref[s + 5:s + 5 + piece, :]
            for kk in range(1, CONV_K):
                xc = xc + cw[kk:kk + 1, :] * xpad_ref[s + 5 + kk:s + 5 + kk + piece, :]
            xc_ref[p0:p0 + piece, :] = xc
        gate_ref[...] = _mm(xc_ref[...], wg_ref[...])
        carry = h_ref[...]
        for p0 in range(0, sub, piece):
            rows = slice(p0, p0 + piece)
            xc = xc_ref[rows, :]
            r = _sigmoid(gate_ref[rows, 0:W] + bg[:, 0:W])
            i = _sigmoid(gate_ref[rows, W:2 * W] + bg[:, W:2 * W])
            a = jnp.exp(neg_c_sp * r)
            m2 = (1.0 - a) * (1.0 + a)
            mult = m2 * lax.rsqrt(jnp.maximum(m2, 1e-30))
            if r0 == 0 and p0 == 0:
                row = lax.broadcasted_iota(jnp.int32, mult.shape, 0)
                mult = jnp.where(jnp.logical_and(first, row == 0), 1.0, mult)
            acum, hloc = _scan_sublane_groups(a, xc * i * mult)
            gy = jax.nn.gelu(y_ref[r0 + p0:r0 + p0 + piece, :])
            for s0 in range(0, piece, SUBLANES):
                g8 = slice(s0, s0 + SUBLANES)
                h = hloc[g8] + acum[g8] * carry
                carry = h[SUBLANES - 1:SUBLANES, :]
                o_ref[r0 + p0 + s0:r0 + p0 + s0 + SUBLANES, :] = h * gy[g8]
        h_ref[...] = carry
    xpad_ref[0:SUBLANES, :] = xpad_ref[tt:tt + SUBLANES, :]


def _rglru(p, B, T, cw, cb, wg, bg, lam, tt=1024, sub=256):
    W = cw.shape[1]
    tt = min(tt, T)
    return pl.pallas_call(
        functools.partial(_rglru_kernel, sub=sub, piece=32),
        grid=(B, T // tt),
        in_specs=[pl.BlockSpec((None, tt, W), lambda b, t: (b, t, 4)),
                  pl.BlockSpec((None, tt, W), lambda b, t: (b, t, 5)),
                  pl.BlockSpec((CONV_K, W), lambda b, t: (0, 0)),
                  pl.BlockSpec((1, W), lambda b, t: (0, 0)),
                  pl.BlockSpec((W, 2 * W), lambda b, t: (0, 0)),
                  pl.BlockSpec((1, 2 * W), lambda b, t: (0, 0)),
                  pl.BlockSpec((1, W), lambda b, t: (0, 0))],
        out_specs=pl.BlockSpec((None, tt, W), lambda b, t: (b, t, 0)),
        out_shape=jax.ShapeDtypeStruct((B, T, W), F32),
        scratch_shapes=[pltpu.VMEM((tt + SUBLANES, W), F32), pltpu.VMEM((1, W), F32),
                        pltpu.VMEM((sub, W), F32), pltpu.VMEM((sub, 2 * W), F32)],
        compiler_params=_params("parallel", "arbitrary"),
        name="rglru",
    )(p, p, cw, cb.reshape(1, W), wg, bg.reshape(1, 2 * W), lam.reshape(1, W))


def _unit_lower_inverse(Ls, diag, same_sub, chunk):
    zero = jnp.zeros((), BF)
    one = jnp.ones((), BF)

    def mmb(a, b):
        return jnp.dot(a, b, preferred_element_type=F32).astype(BF)

    Ld = [jnp.where(same_sub, L, zero) for L in Ls]
    Lo = [jnp.where(same_sub, zero, L) for L in Ls]
    P = [jnp.where(diag, one, -d) for d in Ld]
    M = Ld
    d = 2
    while d < GDN_SUB:
        M = [mmb(m, m) for m in M]
        P = [mmb(p, jnp.where(diag, one, m)) for p, m in zip(P, M)]
        d *= 2
    Nm = [mmb(p, lo) for p, lo in zip(P, Lo)]
    R = [jnp.where(diag, one, -n) for n in Nm]
    M = Nm
    d = 2
    while d < chunk // GDN_SUB:
        M = [mmb(m, m) for m in M]
        R = [mmb(r, jnp.where(diag, one, m)) for r, m in zip(R, M)]
        d *= 2
    return [mmb(r, p) for r, p in zip(R, P)]


def _gdn_kernel(qn_ref, kn_ref, vn_ref, z_ref, ab_ref,
                alog_l_ref, dtb_l_ref, alog_c_ref, dtb_c_ref, nw_ref, o_ref,
                u_ref, wq_ref, qk_ref, kg_ref,
                cd_ref, gcn_ref, bcn_ref, glcn_ref, gnc_ref, s_ref, *, chunk, group,
                heads_per_iter):
    tt, HD = z_ref.shape
    dk = HEAD_DIM
    H = HD // dk
    C, R = chunk, group
    G = tt // R
    CPG = R // C
    t = pl.program_id(1)

    @pl.when(t == 0)
    def _():
        s_ref[...] = jnp.zeros(s_ref.shape, F32)

    ii = lax.broadcasted_iota(jnp.int32, (R, R), 0)
    jj = lax.broadcasted_iota(jnp.int32, (R, R), 1)
    same_chunk = (ii // C) == (jj // C)
    incl = jnp.logical_and(same_chunk, ii >= jj)
    strict = jnp.logical_and(same_chunk, ii > jj)
    same_sub = (ii // GDN_SUB) == (jj // GDN_SUB)
    tril = incl.astype(F32)
    ab = ab_ref[...]
    NL = 2 * H
    g_all = -jnp.exp(alog_l_ref[...]) * _softplus(ab + dtb_l_ref[...])
    beta_all = jax.nn.sigmoid(ab)
    g_cn = jnp.concatenate([g_all[g * R:(g + 1) * R, 0:NL] for g in range(G)], axis=1)
    bcn_ref[...] = jnp.concatenate([beta_all[g * R:(g + 1) * R, 0:NL] for g in range(G)], axis=1)
    gcn_ref[...] = _mm_f32(tril, g_cn)
    glcn_ref[...] = _mm_f32(same_chunk.astype(F32), g_cn)
    a_rows = ab.T[H:NL, :]
    g_rows = -jnp.exp(alog_c_ref[...]) * _softplus(a_rows + dtb_c_ref[...])
    g_nc = jnp.concatenate([g_rows[:, g * R:(g + 1) * R] for g in range(G)], axis=0)
    gnc_ref[...] = _mm_nt_f32(g_nc, tril)

    i64 = lax.broadcasted_iota(jnp.int32, (C, C), 0)
    j64 = lax.broadcasted_iota(jnp.int32, (C, C), 1)
    incl64 = i64 >= j64

    for hp in range(H // heads_per_iter):
        units = [(hp * heads_per_iter + hh, g) for hh in range(heads_per_iter) for g in range(G)]
        qs, ks, kbs, xs, dms, egs, gcols, grows, glcols = [], [], [], [], [], [], [], [], []
        for h, g in units:
            rows = slice(g * R, (g + 1) * R)
            q = qn_ref[h, rows, :]
            k = kn_ref[h, rows, :]
            v = vn_ref[h, rows, :]
            la, lb = g * NL + H + h, g * NL + h
            gcol = jnp.broadcast_to(gcn_ref[:, la:la + 1], (R, dk))
            bcol = jnp.broadcast_to(bcn_ref[:, lb:lb + 1], (R, dk))
            grow = gnc_ref[g * H + h:g * H + h + 1, :]
            eg = jnp.exp(gcol)
            kb = k * bcol
            qs.append(q)
            ks.append(k)
            kbs.append(kb)
            xs.append(jnp.concatenate([v * bcol, kb * eg], axis=1))
            dms.append(jnp.where(incl, jnp.exp(jnp.where(incl, gcol - grow, 0.0)), 0.0))
            egs.append(eg)
            gcols.append(gcol)
            grows.append(grow)
            glcols.append(jnp.broadcast_to(glcn_ref[:, la:la + 1], (R, dk)))
        Ls = [jnp.where(strict, _mm_nt(kb, k) * dm, 0.0).astype(BF)
              for kb, k, dm in zip(kbs, ks, dms)]
        tinvs = _unit_lower_inverse(Ls, ii == jj, same_sub, C)
        uws = [_mm(tinv, x) for tinv, x in zip(tinvs, xs)]
        for i, (h, g) in enumerate(units):
            rows = slice(g * R, (g + 1) * R)
            q, k, gcol, glcol = qs[i], ks[i], gcols[i], glcols[i]
            u_ref[h, rows, :] = uws[i][:, :dk]
            qg = (q * egs[i]).astype(BF)
            w = uws[i][:, dk:].astype(BF)
            kg_ref[h, rows, :] = (k * jnp.exp(glcol - gcol)).astype(BF)
            for c in range(CPG):
                rc = slice(c * C, (c + 1) * C)
                ci = g * CPG + c
                wq_ref[h, 2 * ci * C:(2 * ci + 1) * C, :] = w[rc]
                wq_ref[h, (2 * ci + 1) * C:(2 * ci + 2) * C, :] = qg[rc]
                dmc = jnp.where(incl64, jnp.exp(jnp.where(incl64, gcol[rc, :C] - grows[i][:, rc], 0.0)), 0.0)
                qk_ref[h, ci * C:(ci + 1) * C, :] = (_mm_nt(q[rc], k[rc]) * dmc).astype(BF)
                cd_ref[h, ci:ci + 1, :] = jnp.exp(glcol[c * C:c * C + 1, :])

    nw = nw_ref[...]

    def step(n, carry):
        r = pl.ds(pl.multiple_of(n * C, C), C)
        r2 = pl.ds(pl.multiple_of(n * 2 * C, 2 * C), 2 * C)
        S = [s_ref[h] for h in range(H)]
        wqs = [_mm(wq_ref[h, r2, :], S[h]) for h in range(H)]
        vb = [(u_ref[h, r, :] - wqs[h][:C]).astype(BF) for h in range(H)]
        intra = [_mm(qk_ref[h, r, :], vb[h]) for h in range(H)]
        dS = [_mm_tn(kg_ref[h, r, :], vb[h]) for h in range(H)]
        for h in range(H):
            hl = slice(h * dk, (h + 1) * dk)
            s_ref[h] = S[h] * cd_ref[h, pl.ds(n, 1), :] + dS[h]
            o = wqs[h][C:] + intra[h]
            o = o * lax.rsqrt(jnp.mean(o * o, axis=-1, keepdims=True) + EPS)
            o_ref[r, hl] = o * nw * _silu(z_ref[r, hl])
        return carry

    lax.fori_loop(0, tt // C, step, 0, unroll=True)


def _gdn(qn, kn, vn, z, ab, a_log, dt_bias, norm_w, B, T):
    H, dk, C, R = GDN_HEADS, HEAD_DIM, GDN_CHUNK, GDN_GROUP
    HD = H * dk
    tt = min(GDN_TILE, T)
    NT, G = T // tt, tt // R

    assert R == dk and ab.shape[-1] == LANES
    lane_vec = lambda a: jnp.zeros((1, LANES), F32).at[0, H:2 * H].set(a)
    alog_l, dtb_l = lane_vec(a_log), lane_vec(dt_bias)

    tile = pl.BlockSpec((None, tt, HD), lambda b, t: (b, t, 0))
    heads = pl.BlockSpec((H, tt, dk), lambda b, t: (0, b * NT + t, 0))
    full = lambda shape: pl.BlockSpec(shape, lambda b, t: (0,) * len(shape))
    hbuf = lambda dt: pltpu.VMEM((H, tt, dk), dt)
    return pl.pallas_call(
        functools.partial(_gdn_kernel, chunk=C, group=R, heads_per_iter=GDN_HEADS_PER_ITER),
        grid=(B, NT),
        in_specs=[heads, heads, heads, tile,
                  pl.BlockSpec((None, tt, LANES), lambda b, t: (b, t, 0)),
                  full((1, LANES)), full((1, LANES)), full((H, 1)), full((H, 1)),
                  full((1, dk))],
        out_specs=tile,
        out_shape=jax.ShapeDtypeStruct((B, T, HD), F32),
        scratch_shapes=[hbuf(F32),
                        pltpu.VMEM((H, 2 * tt, dk), BF),
                        pltpu.VMEM((H, tt, C), BF),
                        hbuf(BF),
                        pltpu.VMEM((H, tt // C, dk), F32),
                        pltpu.VMEM((R, 2 * H * G), F32),
                        pltpu.VMEM((R, 2 * H * G), F32),
                        pltpu.VMEM((R, 2 * H * G), F32),
                        pltpu.VMEM((H * G, R), F32),
                        pltpu.VMEM((H, dk, dk), F32)],
        compiler_params=_params("parallel", "arbitrary"),
        name="gated_delta",
    )(qn, kn, vn, z, ab, alog_l, dtb_l, a_log.reshape(H, 1), dt_bias.reshape(H, 1),
      norm_w.reshape(1, dk))


def _block_diag(w):
    n, d, _ = w.shape
    eye = jnp.eye(n, dtype=w.dtype)
    return (eye[:, None, :, None] * w[:, :, None, :]).reshape(n * d, n * d)


def kernel(x, mixer_norm_w, mlp_norm_w, final_norm_w, w_in_even, lru_conv_w, lru_conv_b,
           lru_w_r, lru_b_r, lru_w_i, lru_b_i, lru_lambda, w_out_even, w_in_odd, gdn_conv_w,
           gdn_a_log, gdn_dt_bias, gdn_norm_w, w_out_odd, w_up, w_down):
    B, T, D = x.shape
    M = B * T
    x2 = x.reshape(M, D)

    n_even = w_in_even.shape[2]
    (p0,) = _norm_proj(x2, mixer_norm_w[0], w_in_even[0].astype(BF), (n_even,))
    p0 = p0.reshape(B, T, n_even)
    o_ret = _retention(p0, B, T)
    wg = jnp.concatenate([_block_diag(lru_w_r[0]), _block_diag(lru_w_i[0])], axis=1).astype(BF)
    bg = jnp.concatenate([lru_b_r[0], lru_b_i[0]])
    o_lru = _rglru(p0, B, T, lru_conv_w[0], lru_conv_b[0], wg, bg, lru_lambda[0])
    x2 = _outproj_mlp([o_ret.reshape(M, -1), o_lru.reshape(M, -1)], x2,
                      w_out_even[0].astype(BF), mlp_norm_w[0],
                      w_up[0].astype(BF), w_down[0].astype(BF))

    n_main = 4 * GDN_HEADS * HEAD_DIM
    w1 = w_in_odd[0]
    w1 = jnp.pad(w1, ((0, 0), (0, n_main + LANES - w1.shape[1]))).astype(BF)
    qn, kn, vn, z, ab = _proj_conv(x2, mixer_norm_w[1], w1, gdn_conv_w[0], T)
    seq = lambda a: a.reshape(B, T, a.shape[-1])
    o_gdn = _gdn(qn, kn, vn, seq(z), seq(ab),
                 gdn_a_log[0], gdn_dt_bias[0], gdn_norm_w[0], B, T)
    x2 = _outproj_mlp([o_gdn.reshape(M, -1)], x2, w_out_odd[0].astype(BF), mlp_norm_w[1],
                      w_up[1].astype(BF), w_down[1].astype(BF), fnw=final_norm_w)
    return x2.reshape(B, T, D)
```

```python
import functools

import numpy as np
import jax
import jax.numpy as jnp
from jax import lax
from jax.experimental import pallas as pl
from jax.experimental.pallas import tpu as pltpu

EPS = 1e-6
ROPE_THETA = 10000.0
LRU_C = 8.0
RET_HEADS = 4
GDN_HEADS = 8
LRU_BLOCKS = 8
CONV_K = 4
HEAD_DIM = 128
RET_CHUNK = 128
RET_UNROLL = 32
GDN_CHUNK = 64
GDN_SUB = 16
GDN_GROUP = 128
GDN_TILE = 512
GDN_HEADS_PER_ITER = 4
GDN_CONV_STRIDE = 4
LANES = 128
SUBLANES = 8
VMEM_LIMIT = 56 * 1024 * 1024

BF = jnp.bfloat16
F32 = jnp.float32


def _mm(a, b):
    return jnp.dot(a.astype(BF), b.astype(BF), preferred_element_type=F32)


def _mm_nt(a, b):
    return lax.dot_general(a.astype(BF), b.astype(BF), (((1,), (1,)), ((), ())),
                           preferred_element_type=F32)


def _mm_tn(a, b):
    return lax.dot_general(a.astype(BF), b.astype(BF), (((0,), (0,)), ((), ())),
                           preferred_element_type=F32)


def _mm_f32(a, b):
    return jnp.dot(a, b, preferred_element_type=F32, precision=lax.Precision.HIGHEST)


def _mm_nt_f32(a, b):
    return lax.dot_general(a, b, (((1,), (1,)), ((), ())), preferred_element_type=F32,
                           precision=lax.Precision.HIGHEST)


def _rms(x, w):
    return x * lax.rsqrt(jnp.mean(x * x, axis=-1, keepdims=True) + EPS) * w


def _silu(x):
    hx = 0.5 * x
    return hx + hx * jnp.tanh(hx)


def _sigmoid(x):
    return 0.5 + 0.5 * jnp.tanh(0.5 * x)


def _softplus(x):
    return jnp.maximum(x, 0.0) + jnp.log(1.0 + jnp.exp(-jnp.abs(x)))


def _params(*sem):
    return pltpu.CompilerParams(dimension_semantics=sem, vmem_limit_bytes=VMEM_LIMIT)


def _norm_proj_kernel(x_ref, nw_ref, w_ref, *out_refs, col_chunk):
    hb = _rms(x_ref[...], nw_ref[...]).astype(BF)
    off = 0
    for o_ref in out_refs:
        n = o_ref.shape[-1]
        for j in range(0, n, col_chunk):
            cw = min(col_chunk, n - j)
            o_ref[:, j:j + cw] = jnp.dot(hb, w_ref[:, off + j:off + j + cw],
                                         preferred_element_type=F32).astype(o_ref.dtype)
        off += n


def _norm_proj(x2, nw, w, splits, tm=512):
    M, D = x2.shape
    Nw = w.shape[1]
    assert sum(splits) == Nw and M % tm == 0
    out_shape = [jax.ShapeDtypeStruct((M, n), F32) for n in splits]
    out_specs = [pl.BlockSpec((tm, n), lambda i: (i, 0)) for n in splits]
    return pl.pallas_call(
        functools.partial(_norm_proj_kernel, col_chunk=512),
        grid=(M // tm,),
        in_specs=[pl.BlockSpec((tm, D), lambda i: (i, 0)),
                  pl.BlockSpec((1, D), lambda i: (0, 0)),
                  pl.BlockSpec((D, Nw), lambda i: (0, 0))],
        out_specs=out_specs,
        out_shape=out_shape,
        compiler_params=_params("parallel"),
        name="norm_proj",
    )(x2, nw.reshape(1, D), w)


def _proj_conv_kernel(x_ref, nw_ref, w_ref, cw_ref, q_out, k_out, v_out, z_out, ab_out,
                      raw_ref, tail_ref, *, tiles_per_seq, col_chunk, stride):
    tm, HD = z_out.shape
    dk = HEAD_DIM
    i = pl.program_id(0)

    @pl.when(i % tiles_per_seq == 0)
    def _():
        tail_ref[...] = jnp.zeros(tail_ref.shape, F32)

    hb = _rms(x_ref[...], nw_ref[...]).astype(BF)
    cw = cw_ref[...]
    heads_per_chunk = col_chunk // dk
    block = stride * SUBLANES
    chunk_no = 0
    for idx, (dst, l2, scale) in enumerate(((q_out, True, dk ** -0.5), (k_out, True, 1.0),
                                            (v_out, False, 1.0))):
        for c0 in range(0, HD, col_chunk):
            wc = idx * HD + c0
            res = jnp.dot(hb, w_ref[:, wc:wc + col_chunk], preferred_element_type=F32)
            for hh in range(heads_per_chunk):
                h = c0 // dk + hh
                raw = raw_ref.at[(chunk_no % 2) * heads_per_chunk + hh]
                raw[0:SUBLANES, :] = tail_ref[idx, h]
                raw[SUBLANES:SUBLANES + tm, :] = res[:, hh * dk:(hh + 1) * dk]
                tail_ref[idx, h] = raw[tm:tm + SUBLANES, :]
                taps = [jnp.broadcast_to(cw[kk:kk + 1, wc + hh * dk:wc + (hh + 1) * dk],
                                         (SUBLANES, dk)) for kk in range(CONV_K)]
                for r0 in range(0, tm, block):
                    xs = {m: raw[pl.ds(SUBLANES + r0 + m, SUBLANES, stride=stride), :]
                          for m in range(1 - CONV_K, stride)}
                    for j in range(stride):
                        y = taps[CONV_K - 1] * xs[j]
                        for kk in range(1, CONV_K):
                            y = y + taps[CONV_K - 1 - kk] * xs[j - kk]
                        y = _silu(y)
                        if l2:
                            y = y * (lax.rsqrt(jnp.sum(y * y, axis=-1, keepdims=True) + EPS)
                                     * scale)
                        dst[h, pl.ds(r0 + j, SUBLANES, stride=stride), :] = y
            chunk_no += 1
    for c0 in range(0, HD, 2 * col_chunk):
        wc = 3 * HD + c0
        z_out[:, c0:c0 + 2 * col_chunk] = jnp.dot(hb, w_ref[:, wc:wc + 2 * col_chunk],
                                                  preferred_element_type=F32)
    ab_out[...] = jnp.dot(hb, w_ref[:, 4 * HD:], preferred_element_type=F32)


def _proj_conv(x2, nw, w, conv_w, T, tm=GDN_TILE):
    M, D = x2.shape
    tm = min(tm, T)
    HD = GDN_HEADS * HEAD_DIM
    Nw = w.shape[1]
    assert Nw == 4 * HD + LANES and T % tm == 0
    H, dk = GDN_HEADS, HEAD_DIM
    heads = pl.BlockSpec((H, tm, dk), lambda i: (0, i, 0))
    col_chunk = 2 * dk
    assert tm % (GDN_CONV_STRIDE * SUBLANES) == 0
    return pl.pallas_call(
        functools.partial(_proj_conv_kernel, tiles_per_seq=T // tm, col_chunk=col_chunk,
                          stride=GDN_CONV_STRIDE),
        grid=(M // tm,),
        in_specs=[pl.BlockSpec((tm, D), lambda i: (i, 0)),
                  pl.BlockSpec((1, D), lambda i: (0, 0)),
                  pl.BlockSpec((D, Nw), lambda i: (0, 0)),
                  pl.BlockSpec((CONV_K, 3 * HD), lambda i: (0, 0))],
        out_specs=[heads, heads, heads, pl.BlockSpec((tm, HD), lambda i: (i, 0)),
                   pl.BlockSpec((tm, LANES), lambda i: (i, 0))],
        out_shape=[jax.ShapeDtypeStruct((H, M, dk), F32)] * 3
                  + [jax.ShapeDtypeStruct((M, HD), F32), jax.ShapeDtypeStruct((M, LANES), F32)],
        scratch_shapes=[pltpu.VMEM((2 * (col_chunk // dk), tm + SUBLANES, dk), F32),
                        pltpu.VMEM((3, H, SUBLANES, dk), F32)],
        compiler_params=_params("arbitrary"),
        name="proj_conv",
    )(x2, nw.reshape(1, D), w, conv_w)


def _outproj_mlp_kernel(*refs, n_parts, final, ff_chunk):
    parts = refs[:n_parts]
    x_ref, wout_ref, nw_ref, wup_ref, wdown_ref = refs[n_parts:n_parts + 5]
    fnw_ref = refs[n_parts + 5] if final else None
    o_ref = refs[-1]
    x1 = x_ref[...]
    off = 0
    for p in parts:
        kd = p.shape[-1]
        x1 = x1 + jnp.dot(p[...].astype(BF), wout_ref[off:off + kd, :],
                          preferred_element_type=F32)
        off += kd
    hb = _rms(x1, nw_ref[...]).astype(BF)
    o_ref[...] = x1
    dff = wup_ref.shape[1]
    for j in range(0, dff, ff_chunk):
        u = jnp.dot(hb, wup_ref[:, j:j + ff_chunk], preferred_element_type=F32)
        a = jnp.square(jnp.maximum(u, 0.0)).astype(BF)
        o_ref[...] += jnp.dot(a, wdown_ref[j:j + ff_chunk, :], preferred_element_type=F32)
    if final:
        o_ref[...] = _rms(o_ref[...], fnw_ref[...])


def _outproj_mlp(parts, x2, wout, nw, wup, wdown, fnw=None, tm=512):
    M, D = x2.shape
    dff = wup.shape[1]
    final = fnw is not None
    in_specs = [pl.BlockSpec((tm, p.shape[1]), lambda i: (i, 0)) for p in parts]
    in_specs += [pl.BlockSpec((tm, D), lambda i: (i, 0)),
                 pl.BlockSpec(wout.shape, lambda i: (0, 0)),
                 pl.BlockSpec((1, D), lambda i: (0, 0)),
                 pl.BlockSpec((D, dff), lambda i: (0, 0)),
                 pl.BlockSpec((dff, D), lambda i: (0, 0))]
    args = list(parts) + [x2, wout, nw.reshape(1, D), wup, wdown]
    if final:
        in_specs.append(pl.BlockSpec((1, D), lambda i: (0, 0)))
        args.append(fnw.reshape(1, D))
    return pl.pallas_call(
        functools.partial(_outproj_mlp_kernel, n_parts=len(parts), final=final, ff_chunk=1024),
        grid=(M // tm,),
        in_specs=in_specs,
        out_specs=pl.BlockSpec((tm, D), lambda i: (i, 0)),
        out_shape=jax.ShapeDtypeStruct((M, D), F32),
        compiler_params=_params("parallel"),
        name="outproj_mlp",
    )(*args)


def _retention_kernel(q_ref, k_ref, v_ref, g_ref, cos_ref, sin_ref, dec_ref, qd_ref, kd_ref,
                      cg_ref, o_ref, *, chunk, unroll):
    T, dk = q_ref.shape
    dec = dec_ref[...]
    qd = qd_ref[...]
    kd = kd_ref[...]
    cg = cg_ref[...]
    scale = dk ** -0.5
    half = dk // 2

    def rope(x, cos, sin):
        return x * cos + pltpu.roll(x, half, 1) * sin

    def body(i, S):
        rs = [pl.ds(pl.multiple_of((i * unroll + j) * chunk, chunk), chunk) for j in range(unroll)]
        qs = [rope(q_ref[r, :], cos_ref[r, :], sin_ref[r, :]) for r in rs]
        ks = [rope(k_ref[r, :], cos_ref[r, :], sin_ref[r, :]) * scale for r in rs]
        vs = [v_ref[r, :].astype(BF) for r in rs]
        scores = [(_mm_nt(q, k) * dec).astype(BF) for q, k in zip(qs, ks)]
        kvs = [_mm_tn(k * kd, v) for k, v in zip(ks, vs)]
        intra = [_mm(s, v) for s, v in zip(scores, vs)]
        states = []
        for kv in kvs:
            states.append(S)
            S = cg * S + kv
        inter = [_mm(q * qd, Sj) for q, Sj in zip(qs, states)]
        for r, oi, oc in zip(rs, intra, inter):
            o = oi + oc
            o = o * lax.rsqrt(jnp.mean(o * o, axis=-1, keepdims=True) + EPS)
            o_ref[r, :] = o * _silu(g_ref[r, :])
        return S

    lax.fori_loop(0, T // (chunk * unroll), body, jnp.zeros((dk, dk), F32))


def _retention(p, B, T, chunk=RET_CHUNK):
    H, dk, C = RET_HEADS, HEAD_DIM, chunk
    half = dk // 2
    f32 = np.float32
    inv = np.power(f32(ROPE_THETA), -np.arange(half, dtype=f32) / f32(half)).astype(f32)
    ang = (np.arange(T, dtype=f32)[:, None] * inv[None, :]).astype(f32)
    cos_h = np.cos(ang.astype(np.float64)).astype(f32)
    sin_h = np.sin(ang.astype(np.float64)).astype(f32)
    cos = np.concatenate([cos_h, cos_h], axis=-1)
    sin = np.concatenate([-sin_h, sin_h], axis=-1)
    log_gamma = np.log1p(-np.exp2(-5.0 - np.arange(H, dtype=np.float64)))
    idx = np.arange(C, dtype=np.float64)
    diff = idx[:, None] - idx[None, :]
    causal = diff >= 0
    dec = np.where(causal, np.exp(log_gamma[:, None, None] * np.where(causal, diff, 0.0)), 0.0).astype(f32)
    qd = np.broadcast_to(np.exp(log_gamma[:, None] * (idx[None, :] + 1.0))[:, :, None], (H, C, dk)).astype(f32)
    kd = np.broadcast_to(np.exp(log_gamma[:, None] * (C - 1.0 - idx[None, :]))[:, :, None], (H, C, dk)).astype(f32)
    cg = np.broadcast_to(np.exp(log_gamma * C)[:, None, None], (H, 1, dk)).astype(f32)

    def col(g):
        return pl.BlockSpec((None, T, dk), lambda b, h: (b, 0, g * H + h))

    tab = pl.BlockSpec((T, dk), lambda b, h: (0, 0))
    return pl.pallas_call(
        functools.partial(_retention_kernel, chunk=C, unroll=min(RET_UNROLL, T // C)),
        grid=(B, H),
        in_specs=[col(0), col(1), col(2), col(3), tab, tab,
                  pl.BlockSpec((None, C, C), lambda b, h: (h, 0, 0)),
                  pl.BlockSpec((None, C, dk), lambda b, h: (h, 0, 0)),
                  pl.BlockSpec((None, C, dk), lambda b, h: (h, 0, 0)),
                  pl.BlockSpec((None, 1, dk), lambda b, h: (h, 0, 0))],
        out_specs=pl.BlockSpec((None, T, dk), lambda b, h: (b, 0, h)),
        out_shape=jax.ShapeDtypeStruct((B, T, H * dk), F32),
        compiler_params=_params("parallel", "parallel"),
        name="retention",
    )(p, p, p, p, cos, sin, dec, qd, kd, cg)


def _scan_sublane_groups(a, b):
    pos = lax.broadcasted_iota(jnp.int32, a.shape, 0) % SUBLANES
    d = 1
    while d < SUBLANES:
        keep = pos >= d
        a_s = jnp.where(keep, pltpu.roll(a, d, 0), 1.0)
        b_s = jnp.where(keep, pltpu.roll(b, d, 0), 0.0)
        b = a * b_s + b
        a = a * a_s
        d *= 2
    return a, b


def _rglru_kernel(x_ref, y_ref, cw_ref, cb_ref, wg_ref, bg_ref, lam_ref, o_ref,
                  xpad_ref, h_ref, xc_ref, gate_ref, *, sub, piece):
    tt, W = x_ref.shape
    t = pl.program_id(1)

    @pl.when(t == 0)
    def _():
        xpad_ref[0:SUBLANES, :] = jnp.zeros((SUBLANES, W), F32)
        h_ref[...] = jnp.zeros(h_ref.shape, F32)

    xpad_ref[SUBLANES:SUBLANES + tt, :] = x_ref[...]
    neg_c_sp = -LRU_C * _softplus(-lam_ref[...])
    cw = cw_ref[...]
    cb = cb_ref[...]
    bg = bg_ref[...]
    first = t == 0
    for r0 in range(0, tt, sub):
        for p0 in range(0, sub, piece):
            s = r0 + p0
            xc = cb + cw[0:1, :] * xpad_ref[s + 5:s + 5 + piece, :]
            for kk in range(1, CONV_K):
                xc = xc + cw[kk:kk + 1, :] * xpad_ref[s + 5 + kk:s + 5 + kk + piece, :]
            xc_ref[p0:p0 + piece, :] = xc
        gate_ref[...] = _mm(xc_ref[...], wg_ref[...])
        carry = h_ref[...]
        for p0 in range(0, sub, piece):
            rows = slice(p0, p0 + piece)
            xc = xc_ref[rows, :]
            r = _sigmoid(gate_ref[rows, 0:W] + bg[:, 0:W])
            i = _sigmoid(gate_ref[rows, W:2 * W] + bg[:, W:2 * W])
            a = jnp.exp(neg_c_sp * r)
            m2 = (1.0 - a) * (1.0 + a)
            mult = m2 * lax.rsqrt(jnp.maximum(m2, 1e-30))
            if r0 == 0 and p0 == 0:
                row = lax.broadcasted_iota(jnp.int32, mult.shape, 0)
                mult = jnp.where(jnp.logical_and(first, row == 0), 1.0, mult)
            acum, hloc = _scan_sublane_groups(a, xc * i * mult)
            gy = jax.nn.gelu(y_ref[r0 + p0:r0 + p0 + piece, :])
            for s0 in range(0, piece, SUBLANES):
                g8 = slice(s0, s0 + SUBLANES)
                h = hloc[g8] + acum[g8] * carry
                carry = h[SUBLANES - 1:SUBLANES, :]
                o_ref[r0 + p0 + s0:r0 + p0 + s0 + SUBLANES, :] = h * gy[g8]
        h_ref[...] = carry
    xpad_ref[0:SUBLANES, :] = xpad_ref[tt:tt + SUBLANES, :]


def _rglru(p, B, T, cw, cb, wg, bg, lam, tt=1024, sub=256):
    W = cw.shape[1]
    tt = min(tt, T)
    return pl.pallas_call(
        functools.partial(_rglru_kernel, sub=sub, piece=32),
        grid=(B, T // tt),
        in_specs=[pl.BlockSpec((None, tt, W), lambda b, t: (b, t, 4)),
                  pl.BlockSpec((None, tt, W), lambda b, t: (b, t, 5)),
                  pl.BlockSpec((CONV_K, W), lambda b, t: (0, 0)),
                  pl.BlockSpec((1, W), lambda b, t: (0, 0)),
                  pl.BlockSpec((W, 2 * W), lambda b, t: (0, 0)),
                  pl.BlockSpec((1, 2 * W), lambda b, t: (0, 0)),
                  pl.BlockSpec((1, W), lambda b, t: (0, 0))],
        out_specs=pl.BlockSpec((None, tt, W), lambda b, t: (b, t, 0)),
        out_shape=jax.ShapeDtypeStruct((B, T, W), F32),
        scratch_shapes=[pltpu.VMEM((tt + SUBLANES, W), F32), pltpu.VMEM((1, W), F32),
                        pltpu.VMEM((sub, W), F32), pltpu.VMEM((sub, 2 * W), F32)],
        compiler_params=_params("parallel", "arbitrary"),
        name="rglru",
    )(p, p, cw, cb.reshape(1, W), wg, bg.reshape(1, 2 * W), lam.reshape(1, W))


def _unit_lower_inverse(Ls, diag, same_sub, chunk):
    zero = jnp.zeros((), BF)
    one = jnp.ones((), BF)

    def mmb(a, b):
        return jnp.dot(a, b, preferred_element_type=F32).astype(BF)

    Ld = [jnp.where(same_sub, L, zero) for L in Ls]
    Lo = [jnp.where(same_sub, zero, L) for L in Ls]
    P = [jnp.where(diag, one, -d) for d in Ld]
    M = Ld
    d = 2
    while d < GDN_SUB:
        M = [mmb(m, m) for m in M]
        P = [mmb(p, jnp.where(diag, one, m)) for p, m in zip(P, M)]
        d *= 2
    Nm = [mmb(p, lo) for p, lo in zip(P, Lo)]
    R = [jnp.where(diag, one, -n) for n in Nm]
    M = Nm
    d = 2
    while d < chunk // GDN_SUB:
        M = [mmb(m, m) for m in M]
        R = [mmb(r, jnp.where(diag, one, m)) for r, m in zip(R, M)]
        d *= 2
    return [mmb(r, p) for r, p in zip(R, P)]


def _gdn_kernel(qn_ref, kn_ref, vn_ref, z_ref, ab_ref,
                alog_l_ref, dtb_l_ref, alog_c_ref, dtb_c_ref, nw_ref, o_ref,
                u_ref, wq_ref, qk_ref, kg_ref,
                cd_ref, gcn_ref, bcn_ref, glcn_ref, gnc_ref, s_ref, *, chunk, group,
                heads_per_iter):
    tt, HD = z_ref.shape
    dk = HEAD_DIM
    H = HD // dk
    C, R = chunk, group
    G = tt // R
    CPG = R // C
    t = pl.program_id(1)

    @pl.when(t == 0)
    def _():
        s_ref[...] = jnp.zeros(s_ref.shape, F32)

    ii = lax.broadcasted_iota(jnp.int32, (R, R), 0)
    jj = lax.broadcasted_iota(jnp.int32, (R, R), 1)
    same_chunk = (ii // C) == (jj // C)
    incl = jnp.logical_and(same_chunk, ii >= jj)
    strict = jnp.logical_and(same_chunk, ii > jj)
    same_sub = (ii // GDN_SUB) == (jj // GDN_SUB)
    tril = incl.astype(F32)
    ab = ab_ref[...]
    NL = 2 * H
    g_all = -jnp.exp(alog_l_ref[...]) * _softplus(ab + dtb_l_ref[...])
    beta_all = jax.nn.sigmoid(ab)
    g_cn = jnp.concatenate([g_all[g * R:(g + 1) * R, 0:NL] for g in range(G)], axis=1)
    bcn_ref[...] = jnp.concatenate([beta_all[g * R:(g + 1) * R, 0:NL] for g in range(G)], axis=1)
    gcn_ref[...] = _mm_f32(tril, g_cn)
    glcn_ref[...] = _mm_f32(same_chunk.astype(F32), g_cn)
    a_rows = ab.T[H:NL, :]
    g_rows = -jnp.exp(alog_c_ref[...]) * _softplus(a_rows + dtb_c_ref[...])
    g_nc = jnp.concatenate([g_rows[:, g * R:(g + 1) * R] for g in range(G)], axis=0)
    gnc_ref[...] = _mm_nt_f32(g_nc, tril)

    i64 = lax.broadcasted_iota(jnp.int32, (C, C), 0)
    j64 = lax.broadcasted_iota(jnp.int32, (C, C), 1)
    incl64 = i64 >= j64

    for hp in range(H // heads_per_iter):
        units = [(hp * heads_per_iter + hh, g) for hh in range(heads_per_iter) for g in range(G)]
        qs, ks, kbs, xs, dms, egs, gcols, grows, glcols = [], [], [], [], [], [], [], [], []
        for h, g in units:
            rows = slice(g * R, (g + 1) * R)
            q = qn_ref[h, rows, :]
            k = kn_ref[h, rows, :]
            v = vn_ref[h, rows, :]
            la, lb = g * NL + H + h, g * NL + h
            gcol = jnp.broadcast_to(gcn_ref[:, la:la + 1], (R, dk))
            bcol = jnp.broadcast_to(bcn_ref[:, lb:lb + 1], (R, dk))
            grow = gnc_ref[g * H + h:g * H + h + 1, :]
            eg = jnp.exp(gcol)
            kb = k * bcol
            qs.append(q)
            ks.append(k)
            kbs.append(kb)
            xs.append(jnp.concatenate([v * bcol, kb * eg], axis=1))
            dms.append(jnp.where(incl, jnp.exp(jnp.where(incl, gcol - grow, 0.0)), 0.0))
            egs.append(eg)
            gcols.append(gcol)
            grows.append(grow)
            glcols.append(jnp.broadcast_to(glcn_ref[:, la:la + 1], (R, dk)))
        Ls = [jnp.where(strict, _mm_nt(kb, k) * dm, 0.0).astype(BF)
              for kb, k, dm in zip(kbs, ks, dms)]
        tinvs = _unit_lower_inverse(Ls, ii == jj, same_sub, C)
        uws = [_mm(tinv, x) for tinv, x in zip(tinvs, xs)]
        for i, (h, g) in enumerate(units):
            rows = slice(g * R, (g + 1) * R)
            q, k, gcol, glcol = qs[i], ks[i], gcols[i], glcols[i]
            u_ref[h, rows, :] = uws[i][:, :dk]
            qg = (q * egs[i]).astype(BF)
            w = uws[i][:, dk:].astype(BF)
            kg_ref[h, rows, :] = (k * jnp.exp(glcol - gcol)).astype(BF)
            for c in range(CPG):
                rc = slice(c * C, (c + 1) * C)
                ci = g * CPG + c
                wq_ref[h, 2 * ci * C:(2 * ci + 1) * C, :] = w[rc]
                wq_ref[h, (2 * ci + 1) * C:(2 * ci + 2) * C, :] = qg[rc]
                dmc = jnp.where(incl64, jnp.exp(jnp.where(incl64, gcol[rc, :C] - grows[i][:, rc], 0.0)), 0.0)
                qk_ref[h, ci * C:(ci + 1) * C, :] = (_mm_nt(q[rc], k[rc]) * dmc).astype(BF)
                cd_ref[h, ci:ci + 1, :] = jnp.exp(glcol[c * C:c * C + 1, :])

    nw = nw_ref[...]

    def step(n, carry):
        r = pl.ds(pl.multiple_of(n * C, C), C)
        r2 = pl.ds(pl.multiple_of(n * 2 * C, 2 * C), 2 * C)
        S = [s_ref[h] for h in range(H)]
        wqs = [_mm(wq_ref[h, r2, :], S[h]) for h in range(H)]
        vb = [(u_ref[h, r, :] - wqs[h][:C]).astype(BF) for h in range(H)]
        intra = [_mm(qk_ref[h, r, :], vb[h]) for h in range(H)]
        dS = [_mm_tn(kg_ref[h, r, :], vb[h]) for h in range(H)]
        for h in range(H):
            hl = slice(h * dk, (h + 1) * dk)
            s_ref[h] = S[h] * cd_ref[h, pl.ds(n, 1), :] + dS[h]
            o = wqs[h][C:] + intra[h]
            o = o * lax.rsqrt(jnp.mean(o * o, axis=-1, keepdims=True) + EPS)
            o_ref[r, hl] = o * nw * _silu(z_ref[r, hl])
        return carry

    lax.fori_loop(0, tt // C, step, 0, unroll=True)


def _gdn(qn, kn, vn, z, ab, a_log, dt_bias, norm_w, B, T):
    H, dk, C, R = GDN_HEADS, HEAD_DIM, GDN_CHUNK, GDN_GROUP
    HD = H * dk
    tt = min(GDN_TILE, T)
    NT, G = T // tt, tt // R

    assert R == dk and ab.shape[-1] == LANES
    lane_vec = lambda a: jnp.zeros((1, LANES), F32).at[0, H:2 * H].set(a)
    alog_l, dtb_l = lane_vec(a_log), lane_vec(dt_bias)

    tile = pl.BlockSpec((None, tt, HD), lambda b, t: (b, t, 0))
    heads = pl.BlockSpec((H, tt, dk), lambda b, t: (0, b * NT + t, 0))
    full = lambda shape: pl.BlockSpec(shape, lambda b, t: (0,) * len(shape))
    hbuf = lambda dt: pltpu.VMEM((H, tt, dk), dt)
    return pl.pallas_call(
        functools.partial(_gdn_kernel, chunk=C, group=R, heads_per_iter=GDN_HEADS_PER_ITER),
        grid=(B, NT),
        in_specs=[heads, heads, heads, tile,
                  pl.BlockSpec((None, tt, LANES), lambda b, t: (b, t, 0)),
                  full((1, LANES)), full((1, LANES)), full((H, 1)), full((H, 1)),
                  full((1, dk))],
        out_specs=tile,
        out_shape=jax.ShapeDtypeStruct((B, T, HD), F32),
        scratch_shapes=[hbuf(F32),
                        pltpu.VMEM((H, 2 * tt, dk), BF),
                        pltpu.VMEM((H, tt, C), BF),
                        hbuf(BF),
                        pltpu.VMEM((H, tt // C, dk), F32),
                        pltpu.VMEM((R, 2 * H * G), F32),
                        pltpu.VMEM((R, 2 * H * G), F32),
                        pltpu.VMEM((R, 2 * H * G), F32),
                        pltpu.VMEM((H * G, R), F32),
                        pltpu.VMEM((H, dk, dk), F32)],
        compiler_params=_params("parallel", "arbitrary"),
        name="gated_delta",
    )(qn, kn, vn, z, ab, alog_l, dtb_l, a_log.reshape(H, 1), dt_bias.reshape(H, 1),
      norm_w.reshape(1, dk))


def _block_diag(w):
    n, d, _ = w.shape
    eye = jnp.eye(n, dtype=w.dtype)
    return (eye[:, None, :, None] * w[:, :, None, :]).reshape(n * d, n * d)


def kernel(x, mixer_norm_w, mlp_norm_w, final_norm_w, w_in_even, lru_conv_w, lru_conv_b,
           lru_w_r, lru_b_r, lru_w_i, lru_b_i, lru_lambda, w_out_even, w_in_odd, gdn_conv_w,
           gdn_a_log, gdn_dt_bias, gdn_norm_w, w_out_odd, w_up, w_down):
    B, T, D = x.shape
    M = B * T
    x2 = x.reshape(M, D)

    n_even = w_in_even.shape[2]
    (p0,) = _norm_proj(x2, mixer_norm_w[0], w_in_even[0].astype(BF), (n_even,))
    p0 = p0.reshape(B, T, n_even)
    o_ret = _retention(p0, B, T)
    wg = jnp.concatenate([_block_diag(lru_w_r[0]), _block_diag(lru_w_i[0])], axis=1).astype(BF)
    bg = jnp.concatenate([lru_b_r[0], lru_b_i[0]])
    o_lru = _rglru(p0, B, T, lru_conv_w[0], lru_conv_b[0], wg, bg, lru_lambda[0])
    x2 = _outproj_mlp([o_ret.reshape(M, -1), o_lru.reshape(M, -1)], x2,
                      w_out_even[0].astype(BF), mlp_norm_w[0],
                      w_up[0].astype(BF), w_down[0].astype(BF))

    n_main = 4 * GDN_HEADS * HEAD_DIM
    w1 = w_in_odd[0]
    w1 = jnp.pad(w1, ((0, 0), (0, n_main + LANES - w1.shape[1]))).astype(BF)
    qn, kn, vn, z, ab = _proj_conv(x2, mixer_norm_w[1], w1, gdn_conv_w[0], T)
    seq = lambda a: a.reshape(B, T, a.shape[-1])
    o_gdn = _gdn(qn, kn, vn, seq(z), seq(ab),
                 gdn_a_log[0], gdn_dt_bias[0], gdn_norm_w[0], B, T)
    x2 = _outproj_mlp([o_gdn.reshape(M, -1)], x2, w_out_odd[0].astype(BF), mlp_norm_w[1],
                      w_up[1].astype(BF), w_down[1].astype(BF), fnw=final_norm_w)
    return x2.reshape(B, T, D)
```

```python
import functools

import numpy as np
import jax
import jax.numpy as jnp
from jax import lax
from jax.experimental import pallas as pl
from jax.experimental.pallas import tpu as pltpu

EPS = 1e-6
ROPE_THETA = 10000.0
LRU_C = 8.0
RET_HEADS = 4
GDN_HEADS = 8
LRU_BLOCKS = 8
CONV_K = 4
HEAD_DIM = 128
RET_CHUNK = 128
RET_UNROLL = 8
GDN_CHUNK = 64
GDN_SUB = 16
GDN_GROUP = 128
GDN_TILE = 512
GDN_HEADS_PER_ITER = 4
GDN_CONV_STRIDE = 4
LANES = 128
SUBLANES = 8
VMEM_LIMIT = 56 * 1024 * 1024

BF = jnp.bfloat16
F32 = jnp.float32


def _mm(a, b):
    return jnp.dot(a.astype(BF), b.astype(BF), preferred_element_type=F32)


def _mm_nt(a, b):
    return lax.dot_general(a.astype(BF), b.astype(BF), (((1,), (1,)), ((), ())),
                           preferred_element_type=F32)


def _mm_tn(a, b):
    return lax.dot_general(a.astype(BF), b.astype(BF), (((0,), (0,)), ((), ())),
                           preferred_element_type=F32)


def _mm_f32(a, b):
    return jnp.dot(a, b, preferred_element_type=F32, precision=lax.Precision.HIGHEST)


def _mm_nt_f32(a, b):
    return lax.dot_general(a, b, (((1,), (1,)), ((), ())), preferred_element_type=F32,
                           precision=lax.Precision.HIGHEST)


def _rms(x, w):
    return x * lax.rsqrt(jnp.mean(x * x, axis=-1, keepdims=True) + EPS) * w


def _silu(x):
    hx = 0.5 * x
    return hx + hx * jnp.tanh(hx)


def _sigmoid(x):
    return 0.5 + 0.5 * jnp.tanh(0.5 * x)


def _softplus(x):
    return jnp.maximum(x, 0.0) + jnp.log(1.0 + jnp.exp(-jnp.abs(x)))


def _params(*sem):
    return pltpu.CompilerParams(dimension_semantics=sem, vmem_limit_bytes=VMEM_LIMIT)


def _norm_proj_kernel(x_ref, nw_ref, w_ref, *out_refs, col_chunk):
    hb = _rms(x_ref[...], nw_ref[...]).astype(BF)
    off = 0
    for o_ref in out_refs:
        n = o_ref.shape[-1]
        for j in range(0, n, col_chunk):
            cw = min(col_chunk, n - j)
            o_ref[:, j:j + cw] = jnp.dot(hb, w_ref[:, off + j:off + j + cw],
                                         preferred_element_type=F32).astype(o_ref.dtype)
        off += n


def _norm_proj(x2, nw, w, splits, tm=512):
    M, D = x2.shape
    Nw = w.shape[1]
    assert sum(splits) == Nw and M % tm == 0
    out_shape = [jax.ShapeDtypeStruct((M, n), F32) for n in splits]
    out_specs = [pl.BlockSpec((tm, n), lambda i: (i, 0)) for n in splits]
    return pl.pallas_call(
        functools.partial(_norm_proj_kernel, col_chunk=512),
        grid=(M // tm,),
        in_specs=[pl.BlockSpec((tm, D), lambda i: (i, 0)),
                  pl.BlockSpec((1, D), lambda i: (0, 0)),
                  pl.BlockSpec((D, Nw), lambda i: (0, 0))],
        out_specs=out_specs,
        out_shape=out_shape,
        compiler_params=_params("parallel"),
        name="norm_proj",
    )(x2, nw.reshape(1, D), w)


def _proj_conv_kernel(x_ref, nw_ref, w_ref, cw_ref, q_out, k_out, v_out, z_out, ab_out,
                      raw_ref, tail_ref, *, tiles_per_seq, col_chunk, stride):
    tm, HD = z_out.shape
    dk = HEAD_DIM
    i = pl.program_id(0)

    @pl.when(i % tiles_per_seq == 0)
    def _():
        tail_ref[...] = jnp.zeros(tail_ref.shape, F32)

    hb = _rms(x_ref[...], nw_ref[...]).astype(BF)
    cw = cw_ref[...]
    heads_per_chunk = col_chunk // dk
    block = stride * SUBLANES
    chunk_no = 0
    for idx, (dst, l2, scale) in enumerate(((q_out, True, dk ** -0.5), (k_out, True, 1.0),
                                            (v_out, False, 1.0))):
        for c0 in range(0, HD, col_chunk):
            wc = idx * HD + c0
            res = jnp.dot(hb, w_ref[:, wc:wc + col_chunk], preferred_element_type=F32)
            for hh in range(heads_per_chunk):
                h = c0 // dk + hh
                raw = raw_ref.at[(chunk_no % 2) * heads_per_chunk + hh]
                raw[0:SUBLANES, :] = tail_ref[idx, h]
                raw[SUBLANES:SUBLANES + tm, :] = res[:, hh * dk:(hh + 1) * dk]
                tail_ref[idx, h] = raw[tm:tm + SUBLANES, :]
                taps = [jnp.broadcast_to(cw[kk:kk + 1, wc + hh * dk:wc + (hh + 1) * dk],
                                         (SUBLANES, dk)) for kk in range(CONV_K)]
                for r0 in range(0, tm, block):
                    xs = {m: raw[pl.ds(SUBLANES + r0 + m, SUBLANES, stride=stride), :]
                          for m in range(1 - CONV_K, stride)}
                    for j in range(stride):
                        y = taps[CONV_K - 1] * xs[j]
                        for kk in range(1, CONV_K):
                            y = y + taps[CONV_K - 1 - kk] * xs[j - kk]
                        y = _silu(y)
                        if l2:
                            y = y * (lax.rsqrt(jnp.sum(y * y, axis=-1, keepdims=True) + EPS)
                                     * scale)
                        dst[h, pl.ds(r0 + j, SUBLANES, stride=stride), :] = y
            chunk_no += 1
    for c0 in range(0, HD, 2 * col_chunk):
        wc = 3 * HD + c0
        z_out[:, c0:c0 + 2 * col_chunk] = jnp.dot(hb, w_ref[:, wc:wc + 2 * col_chunk],
                                                  preferred_element_type=F32)
    ab_out[...] = jnp.dot(hb, w_ref[:, 4 * HD:], preferred_element_type=F32)


def _proj_conv(x2, nw, w, conv_w, T, tm=GDN_TILE):
    M, D = x2.shape
    tm = min(tm, T)
    HD = GDN_HEADS * HEAD_DIM
    Nw = w.shape[1]
    assert Nw == 4 * HD + LANES and T % tm == 0
    H, dk = GDN_HEADS, HEAD_DIM
    heads = pl.BlockSpec((H, tm, dk), lambda i: (0, i, 0))
    col_chunk = 2 * dk
    assert tm % (GDN_CONV_STRIDE * SUBLANES) == 0
    return pl.pallas_call(
        functools.partial(_proj_conv_kernel, tiles_per_seq=T // tm, col_chunk=col_chunk,
                          stride=GDN_CONV_STRIDE),
        grid=(M // tm,),
        in_specs=[pl.BlockSpec((tm, D), lambda i: (i, 0)),
                  pl.BlockSpec((1, D), lambda i: (0, 0)),
                  pl.BlockSpec((D, Nw), lambda i: (0, 0)),
                  pl.BlockSpec((CONV_K, 3 * HD), lambda i: (0, 0))],
        out_specs=[heads, heads, heads, pl.BlockSpec((tm, HD), lambda i: (i, 0)),
                   pl.BlockSpec((tm, LANES), lambda i: (i, 0))],
        out_shape=[jax.ShapeDtypeStruct((H, M, dk), F32)] * 3
                  + [jax.ShapeDtypeStruct((M, HD), F32), jax.ShapeDtypeStruct((M, LANES), F32)],
        scratch_shapes=[pltpu.VMEM((2 * (col_chunk // dk), tm + SUBLANES, dk), F32),
                        pltpu.VMEM((3, H, SUBLANES, dk), F32)],
        compiler_params=_params("arbitrary"),
        name="proj_conv",
    )(x2, nw.reshape(1, D), w, conv_w)


def _outproj_mlp_kernel(*refs, n_parts, final, ff_chunk):
    parts = refs[:n_parts]
    x_ref, wout_ref, nw_ref, wup_ref, wdown_ref = refs[n_parts:n_parts + 5]
    fnw_ref = refs[n_parts + 5] if final else None
    o_ref = refs[-1]
    x1 = x_ref[...]
    off = 0
    for p in parts:
        kd = p.shape[-1]
        x1 = x1 + jnp.dot(p[...].astype(BF), wout_ref[off:off + kd, :],
                          preferred_element_type=F32)
        off += kd
    hb = _rms(x1, nw_ref[...]).astype(BF)
    o_ref[...] = x1
    dff = wup_ref.shape[1]
    for j in range(0, dff, ff_chunk):
        u = jnp.dot(hb, wup_ref[:, j:j + ff_chunk], preferred_element_type=F32)
        a = jnp.square(jnp.maximum(u, 0.0)).astype(BF)
        o_ref[...] += jnp.dot(a, wdown_ref[j:j + ff_chunk, :], preferred_element_type=F32)
    if final:
        o_ref[...] = _rms(o_ref[...], fnw_ref[...])


def _outproj_mlp(parts, x2, wout, nw, wup, wdown, fnw=None, tm=512):
    M, D = x2.shape
    dff = wup.shape[1]
    final = fnw is not None
    in_specs = [pl.BlockSpec((tm, p.shape[1]), lambda i: (i, 0)) for p in parts]
    in_specs += [pl.BlockSpec((tm, D), lambda i: (i, 0)),
                 pl.BlockSpec(wout.shape, lambda i: (0, 0)),
                 pl.BlockSpec((1, D), lambda i: (0, 0)),
                 pl.BlockSpec((D, dff), lambda i: (0, 0)),
                 pl.BlockSpec((dff, D), lambda i: (0, 0))]
    args = list(parts) + [x2, wout, nw.reshape(1, D), wup, wdown]
    if final:
        in_specs.append(pl.BlockSpec((1, D), lambda i: (0, 0)))
        args.append(fnw.reshape(1, D))
    return pl.pallas_call(
        functools.partial(_outproj_mlp_kernel, n_parts=len(parts), final=final, ff_chunk=1024),
        grid=(M // tm,),
        in_specs=in_specs,
        out_specs=pl.BlockSpec((tm, D), lambda i: (i, 0)),
        out_shape=jax.ShapeDtypeStruct((M, D), F32),
        compiler_params=_params("parallel"),
        name="outproj_mlp",
    )(*args)


def _retention_kernel(q_ref, k_ref, v_ref, g_ref, cos_ref, sin_ref, dec_ref, qd_ref, kd_ref,
                      cg_ref, o_ref, *, chunk, unroll):
    T, dk = q_ref.shape
    dec = dec_ref[...]
    qd = qd_ref[...]
    kd = kd_ref[...]
    cg = cg_ref[...]
    scale = dk ** -0.5
    half = dk // 2

    def rope(x, cos, sin):
        return x * cos + pltpu.roll(x, half, 1) * sin

    def body(i, S):
        rs = [pl.ds(pl.multiple_of((i * unroll + j) * chunk, chunk), chunk) for j in range(unroll)]
        qs = [rope(q_ref[r, :], cos_ref[r, :], sin_ref[r, :]) for r in rs]
        ks = [rope(k_ref[r, :], cos_ref[r, :], sin_ref[r, :]) * scale for r in rs]
        vs = [v_ref[r, :].astype(BF) for r in rs]
        scores = [(_mm_nt(q, k) * dec).astype(BF) for q, k in zip(qs, ks)]
        kvs = [_mm_tn(k * kd, v) for k, v in zip(ks, vs)]
        intra = [_mm(s, v) for s, v in zip(scores, vs)]
        states = []
        for kv in kvs:
            states.append(S)
            S = cg * S + kv
        inter = [_mm(q * qd, Sj) for q, Sj in zip(qs, states)]
        for r, oi, oc in zip(rs, intra, inter):
            o = oi + oc
            o = o * lax.rsqrt(jnp.mean(o * o, axis=-1, keepdims=True) + EPS)
            o_ref[r, :] = (o * _silu(g_ref[r, :])).astype(o_ref.dtype)
        return S

    lax.fori_loop(0, T // (chunk * unroll), body, jnp.zeros((dk, dk), F32))


def _retention(p, B, T, chunk=RET_CHUNK):
    H, dk, C = RET_HEADS, HEAD_DIM, chunk
    half = dk // 2
    f32 = np.float32
    inv = np.power(f32(ROPE_THETA), -np.arange(half, dtype=f32) / f32(half)).astype(f32)
    ang = (np.arange(T, dtype=f32)[:, None] * inv[None, :]).astype(f32)
    cos_h = np.cos(ang.astype(np.float64)).astype(f32)
    sin_h = np.sin(ang.astype(np.float64)).astype(f32)
    cos = np.concatenate([cos_h, cos_h], axis=-1)
    sin = np.concatenate([-sin_h, sin_h], axis=-1)
    log_gamma = np.log1p(-np.exp2(-5.0 - np.arange(H, dtype=np.float64)))
    idx = np.arange(C, dtype=np.float64)
    diff = idx[:, None] - idx[None, :]
    causal = diff >= 0
    dec = np.where(causal, np.exp(log_gamma[:, None, None] * np.where(causal, diff, 0.0)), 0.0).astype(f32)
    qd = np.broadcast_to(np.exp(log_gamma[:, None] * (idx[None, :] + 1.0))[:, :, None], (H, C, dk)).astype(f32)
    kd = np.broadcast_to(np.exp(log_gamma[:, None] * (C - 1.0 - idx[None, :]))[:, :, None], (H, C, dk)).astype(f32)
    cg = np.broadcast_to(np.exp(log_gamma * C)[:, None, None], (H, 1, dk)).astype(f32)

    def col(g):
        return pl.BlockSpec((None, T, dk), lambda b, h: (b, 0, g * H + h))

    tab = pl.BlockSpec((T, dk), lambda b, h: (0, 0))
    return pl.pallas_call(
        functools.partial(_retention_kernel, chunk=C, unroll=min(RET_UNROLL, T // C)),
        grid=(B, H),
        in_specs=[col(0), col(1), col(2), col(3), tab, tab,
                  pl.BlockSpec((None, C, C), lambda b, h: (h, 0, 0)),
                  pl.BlockSpec((None, C, dk), lambda b, h: (h, 0, 0)),
                  pl.BlockSpec((None, C, dk), lambda b, h: (h, 0, 0)),
                  pl.BlockSpec((None, 1, dk), lambda b, h: (h, 0, 0))],
        out_specs=pl.BlockSpec((None, T, dk), lambda b, h: (b, 0, h)),
        out_shape=jax.ShapeDtypeStruct((B, T, H * dk), BF),
        compiler_params=_params("parallel", "parallel"),
        name="retention",
    )(p, p, p, p, cos, sin, dec, qd, kd, cg)


def _scan_sublane_groups(a, b):
    pos = lax.broadcasted_iota(jnp.int32, a.shape, 0) % SUBLANES
    d = 1
    while d < SUBLANES:
        keep = pos >= d
        a_s = jnp.where(keep, pltpu.roll(a, d, 0), 1.0)
        b_s = jnp.where(keep, pltpu.roll(b, d, 0), 0.0)
        b = a * b_s + b
        a = a * a_s
        d *= 2
    return a, b


def _rglru_kernel(x_ref, y_ref, cw_ref, cb_ref, wg_ref, bg_ref, lam_ref, o_ref,
                  xpad_ref, h_ref, xc_ref, gate_ref, *, sub, piece):
    tt, W = x_ref.shape
    t = pl.program_id(1)

    @pl.when(t == 0)
    def _():
        xpad_ref[0:SUBLANES, :] = jnp.zeros((SUBLANES, W), F32)
        h_ref[...] = jnp.zeros(h_ref.shape, F32)

    xpad_ref[SUBLANES:SUBLANES + tt, :] = x_ref[...]
    neg_c_sp = -LRU_C * _softplus(-lam_ref[...])
    cw = cw_ref[...]
    cb = cb_ref[...]
    bg = bg_ref[...]
    first = t == 0
    for r0 in range(0, tt, sub):
        for p0 in range(0, sub, piece):
            s = r0 + p0
            xc = cb + cw[0:1, :] * xpad_ref[s + 5:s + 5 + piece, :]
            for kk in range(1, CONV_K):
                xc = xc + cw[kk:kk + 1, :] * xpad_ref[s + 5 + kk:s + 5 + kk + piece, :]
            xc_ref[p0:p0 + piece, :] = xc
        gate_ref[...] = _mm(xc_ref[...], wg_ref[...])
        carry = h_ref[...]
        for p0 in range(0, sub, piece):
            rows = slice(p0, p0 + piece)
            xc = xc_ref[rows, :]
            r = _sigmoid(gate_ref[rows, 0:W] + bg[:, 0:W])
            i = _sigmoid(gate_ref[rows, W:2 * W] + bg[:, W:2 * W])
            a = jnp.exp(neg_c_sp * r)
            m2 = (1.0 - a) * (1.0 + a)
            mult = m2 * lax.rsqrt(jnp.maximum(m2, 1e-30))
            if r0 == 0 and p0 == 0:
                row = lax.broadcasted_iota(jnp.int32, mult.shape, 0)
                mult = jnp.where(jnp.logical_and(first, row == 0), 1.0, mult)
            acum, hloc = _scan_sublane_groups(a, xc * i * mult)
            gy = jax.nn.gelu(y_ref[r0 + p0:r0 + p0 + piece, :])
            for s0 in range(0, piece, SUBLANES):
                g8 = slice(s0, s0 + SUBLANES)
                h = hloc[g8] + acum[g8] * carry
                carry = h[SUBLANES - 1:SUBLANES, :]
                o_ref[r0 + p0 + s0:r0 + p0 + s0 + SUBLANES, :] = h * gy[g8]
        h_ref[...] = carry
    xpad_ref[0:SUBLANES, :] = xpad_ref[tt:tt + SUBLANES, :]


def _rglru(p, B, T, cw, cb, wg, bg, lam, tt=1024, sub=256):
    W = cw.shape[1]
    tt = min(tt, T)
    return pl.pallas_call(
        functools.partial(_rglru_kernel, sub=sub, piece=32),
        grid=(B, T // tt),
        in_specs=[pl.BlockSpec((None, tt, W), lambda b, t: (b, t, 4)),
                  pl.BlockSpec((None, tt, W), lambda b, t: (b, t, 5)),
                  pl.BlockSpec((CONV_K, W), lambda b, t: (0, 0)),
                  pl.BlockSpec((1, W), lambda b, t: (0, 0)),
                  pl.BlockSpec((W, 2 * W), lambda b, t: (0, 0)),
                  pl.BlockSpec((1, 2 * W), lambda b, t: (0, 0)),
                  pl.BlockSpec((1, W), lambda b, t: (0, 0))],
        out_specs=pl.BlockSpec((None, tt, W), lambda b, t: (b, t, 0)),
        out_shape=jax.ShapeDtypeStruct((B, T, W), F32),
        scratch_shapes=[pltpu.VMEM((tt + SUBLANES, W), F32), pltpu.VMEM((1, W), F32),
                        pltpu.VMEM((sub, W), F32), pltpu.VMEM((sub, 2 * W), F32)],
        compiler_params=_params("parallel", "arbitrary"),
        name="rglru",
    )(p, p, cw, cb.reshape(1, W), wg, bg.reshape(1, 2 * W), lam.reshape(1, W))


def _unit_lower_inverse(Ls, diag, same_sub, chunk):
    zero = jnp.zeros((), BF)
    one = jnp.ones((), BF)

    def mmb(a, b):
        return jnp.dot(a, b, preferred_element_type=F32).astype(BF)

    Ld = [jnp.where(same_sub, L, zero) for L in Ls]
    Lo = [jnp.where(same_sub, zero, L) for L in Ls]
    P = [jnp.where(diag, one, -d) for d in Ld]
    M = Ld
    d = 2
    while d < GDN_SUB:
        M = [mmb(m, m) for m in M]
        P = [mmb(p, jnp.where(diag, one, m)) for p, m in zip(P, M)]
        d *= 2
    Nm = [mmb(p, lo) for p, lo in zip(P, Lo)]
    R = [jnp.where(diag, one, -n) for n in Nm]
    M = Nm
    d = 2
    while d < chunk // GDN_SUB:
        M = [mmb(m, m) for m in M]
        R = [mmb(r, jnp.where(diag, one, m)) for r, m in zip(R, M)]
        d *= 2
    return [mmb(r, p) for r, p in zip(R, P)]


def _gdn_kernel(qn_ref, kn_ref, vn_ref, z_ref, ab_ref,
                alog_l_ref, dtb_l_ref, alog_c_ref, dtb_c_ref, nw_ref, o_ref,
                u_ref, wq_ref, qk_ref, kg_ref,
                cd_ref, gcn_ref, bcn_ref, glcn_ref, gnc_ref, s_ref, *, chunk, group,
                heads_per_iter):
    tt, HD = z_ref.shape
    dk = HEAD_DIM
    H = HD // dk
    C, R = chunk, group
    G = tt // R
    CPG = R // C
    t = pl.program_id(1)

    @pl.when(t == 0)
    def _():
        s_ref[...] = jnp.zeros(s_ref.shape, F32)

    ii = lax.broadcasted_iota(jnp.int32, (R, R), 0)
    jj = lax.broadcasted_iota(jnp.int32, (R, R), 1)
    same_chunk = (ii // C) == (jj // C)
    incl = jnp.logical_and(same_chunk, ii >= jj)
    strict = jnp.logical_and(same_chunk, ii > jj)
    same_sub = (ii // GDN_SUB) == (jj // GDN_SUB)
    tril = incl.astype(F32)
    ab = ab_ref[...]
    NL = 2 * H
    g_all = -jnp.exp(alog_l_ref[...]) * _softplus(ab + dtb_l_ref[...])
    beta_all = jax.nn.sigmoid(ab)
    g_cn = jnp.concatenate([g_all[g * R:(g + 1) * R, 0:NL] for g in range(G)], axis=1)
    bcn_ref[...] = jnp.concatenate([beta_all[g * R:(g + 1) * R, 0:NL] for g in range(G)], axis=1)
    gcn_ref[...] = _mm_f32(tril, g_cn)
    glcn_ref[...] = _mm_f32(same_chunk.astype(F32), g_cn)
    a_rows = ab.T[H:NL, :]
    g_rows = -jnp.exp(alog_c_ref[...]) * _softplus(a_rows + dtb_c_ref[...])
    g_nc = jnp.concatenate([g_rows[:, g * R:(g + 1) * R] for g in range(G)], axis=0)
    gnc_ref[...] = _mm_nt_f32(g_nc, tril)

    i64 = lax.broadcasted_iota(jnp.int32, (C, C), 0)
    j64 = lax.broadcasted_iota(jnp.int32, (C, C), 1)
    incl64 = i64 >= j64

    for hp in range(H // heads_per_iter):
        units = [(hp * heads_per_iter + hh, g) for hh in range(heads_per_iter) for g in range(G)]
        qs, ks, kbs, xs, dms, egs, gcols, grows, glcols = [], [], [], [], [], [], [], [], []
        for h, g in units:
            rows = slice(g * R, (g + 1) * R)
            q = qn_ref[h, rows, :]
            k = kn_ref[h, rows, :]
            v = vn_ref[h, rows, :]
            la, lb = g * NL + H + h, g * NL + h
            gcol = jnp.broadcast_to(gcn_ref[:, la:la + 1], (R, dk))
            bcol = jnp.broadcast_to(bcn_ref[:, lb:lb + 1], (R, dk))
            grow = gnc_ref[g * H + h:g * H + h + 1, :]
            eg = jnp.exp(gcol)
            kb = k * bcol
            qs.append(q)
            ks.append(k)
            kbs.append(kb)
            xs.append(jnp.concatenate([v * bcol, kb * eg], axis=1))
            dms.append(jnp.where(incl, jnp.exp(jnp.where(incl, gcol - grow, 0.0)), 0.0))
            egs.append(eg)
            gcols.append(gcol)
            grows.append(grow)
            glcols.append(jnp.broadcast_to(glcn_ref[:, la:la + 1], (R, dk)))
        Ls = [jnp.where(strict, _mm_nt(kb, k) * dm, 0.0).astype(BF)
              for kb, k, dm in zip(kbs, ks, dms)]
        tinvs = _unit_lower_inverse(Ls, ii == jj, same_sub, C)
        uws = [_mm(tinv, x) for tinv, x in zip(tinvs, xs)]
        for i, (h, g) in enumerate(units):
            rows = slice(g * R, (g + 1) * R)
            q, k, gcol, glcol = qs[i], ks[i], gcols[i], glcols[i]
            u_ref[h, rows, :] = uws[i][:, :dk]
            qg = (q * egs[i]).astype(BF)
            w = uws[i][:, dk:].astype(BF)
            kg_ref[h, rows, :] = (k * jnp.exp(glcol - gcol)).astype(BF)
            for c in range(CPG):
                rc = slice(c * C, (c + 1) * C)
                ci = g * CPG + c
                wq_ref[h, 2 * ci * C:(2 * ci + 1) * C, :] = w[rc]
                wq_ref[h, (2 * ci + 1) * C:(2 * ci + 2) * C, :] = qg[rc]
                dmc = jnp.where(incl64, jnp.exp(jnp.where(incl64, gcol[rc, :C] - grows[i][:, rc], 0.0)), 0.0)
                qk_ref[h, ci * C:(ci + 1) * C, :] = (_mm_nt(q[rc], k[rc]) * dmc).astype(BF)
                cd_ref[h, ci:ci + 1, :] = jnp.exp(glcol[c * C:c * C + 1, :])

    nw = nw_ref[...]

    def step(n, carry):
        r = pl.ds(pl.multiple_of(n * C, C), C)
        r2 = pl.ds(pl.multiple_of(n * 2 * C, 2 * C), 2 * C)
        S = [s_ref[h] for h in range(H)]
        wqs = [_mm(wq_ref[h, r2, :], S[h]) for h in range(H)]
        vb = [(u_ref[h, r, :] - wqs[h][:C]).astype(BF) for h in range(H)]
        intra = [_mm(qk_ref[h, r, :], vb[h]) for h in range(H)]
        dS = [_mm_tn(kg_ref[h, r, :], vb[h]) for h in range(H)]
        for h in range(H):
            hl = slice(h * dk, (h + 1) * dk)
            s_ref[h] = S[h] * cd_ref[h, pl.ds(n, 1), :] + dS[h]
            o = wqs[h][C:] + intra[h]
            o = o * lax.rsqrt(jnp.mean(o * o, axis=-1, keepdims=True) + EPS)
            o_ref[r, hl] = (o * nw * _silu(z_ref[r, hl])).astype(o_ref.dtype)
        return carry

    lax.fori_loop(0, tt // C, step, 0, unroll=True)


def _gdn(qn, kn, vn, z, ab, a_log, dt_bias, norm_w, B, T):
    H, dk, C, R = GDN_HEADS, HEAD_DIM, GDN_CHUNK, GDN_GROUP
    HD = H * dk
    tt = min(GDN_TILE, T)
    NT, G = T // tt, tt // R

    assert R == dk and ab.shape[-1] == LANES
    lane_vec = lambda a: jnp.zeros((1, LANES), F32).at[0, H:2 * H].set(a)
    alog_l, dtb_l = lane_vec(a_log), lane_vec(dt_bias)

    tile = pl.BlockSpec((None, tt, HD), lambda b, t: (b, t, 0))
    heads = pl.BlockSpec((H, tt, dk), lambda b, t: (0, b * NT + t, 0))
    full = lambda shape: pl.BlockSpec(shape, lambda b, t: (0,) * len(shape))
    hbuf = lambda dt: pltpu.VMEM((H, tt, dk), dt)
    return pl.pallas_call(
        functools.partial(_gdn_kernel, chunk=C, group=R, heads_per_iter=GDN_HEADS_PER_ITER),
        grid=(B, NT),
        in_specs=[heads, heads, heads, tile,
                  pl.BlockSpec((None, tt, LANES), lambda b, t: (b, t, 0)),
                  full((1, LANES)), full((1, LANES)), full((H, 1)), full((H, 1)),
                  full((1, dk))],
        out_specs=tile,
        out_shape=jax.ShapeDtypeStruct((B, T, HD), BF),
        scratch_shapes=[hbuf(F32),
                        pltpu.VMEM((H, 2 * tt, dk), BF),
                        pltpu.VMEM((H, tt, C), BF),
                        hbuf(BF),
                        pltpu.VMEM((H, tt // C, dk), F32),
                        pltpu.VMEM((R, 2 * H * G), F32),
                        pltpu.VMEM((R, 2 * H * G), F32),
                        pltpu.VMEM((R, 2 * H * G), F32),
                        pltpu.VMEM((H * G, R), F32),
                        pltpu.VMEM((H, dk, dk), F32)],
        compiler_params=_params("parallel", "arbitrary"),
        name="gated_delta",
    )(qn, kn, vn, z, ab, alog_l, dtb_l, a_log.reshape(H, 1), dt_bias.reshape(H, 1),
      norm_w.reshape(1, dk))


def _block_diag(w):
    n, d, _ = w.shape
    eye = jnp.eye(n, dtype=w.dtype)
    return (eye[:, None, :, None] * w[:, :, None, :]).reshape(n * d, n * d)


def kernel(x, mixer_norm_w, mlp_norm_w, final_norm_w, w_in_even, lru_conv_w, lru_conv_b,
           lru_w_r, lru_b_r, lru_w_i, lru_b_i, lru_lambda, w_out_even, w_in_odd, gdn_conv_w,
           gdn_a_log, gdn_dt_bias, gdn_norm_w, w_out_odd, w_up, w_down):
    B, T, D = x.shape
    M = B * T
    x2 = x.reshape(M, D)

    n_even = w_in_even.shape[2]
    (p0,) = _norm_proj(x2, mixer_norm_w[0], w_in_even[0].astype(BF), (n_even,))
    p0 = p0.reshape(B, T, n_even)
    o_ret = _retention(p0, B, T)
    wg = jnp.concatenate([_block_diag(lru_w_r[0]), _block_diag(lru_w_i[0])], axis=1).astype(BF)
    bg = jnp.concatenate([lru_b_r[0], lru_b_i[0]])
    o_lru = _rglru(p0, B, T, lru_conv_w[0], lru_conv_b[0], wg, bg, lru_lambda[0])
    x2 = _outproj_mlp([o_ret.reshape(M, -1), o_lru.reshape(M, -1)], x2,
                      w_out_even[0].astype(BF), mlp_norm_w[0],
                      w_up[0].astype(BF), w_down[0].astype(BF))

    n_main = 4 * GDN_HEADS * HEAD_DIM
    w1 = w_in_odd[0]
    w1 = jnp.pad(w1, ((0, 0), (0, n_main + LANES - w1.shape[1]))).astype(BF)
    qn, kn, vn, z, ab = _proj_conv(x2, mixer_norm_w[1], w1, gdn_conv_w[0], T)
    seq = lambda a: a.reshape(B, T, a.shape[-1])
    o_gdn = _gdn(qn, kn, vn, seq(z), seq(ab),
                 gdn_a_log[0], gdn_dt_bias[0], gdn_norm_w[0], B, T)
    x2 = _outproj_mlp([o_gdn.reshape(M, -1)], x2, w_out_odd[0].astype(BF), mlp_norm_w[1],
                      w_up[1].astype(BF), w_down[1].astype(BF), fnw=final_norm_w)
    return x2.reshape(B, T, D)
```
